```python
import math
import jax, jax.numpy as jnp
from jax import lax
import numpy as np

D_MODEL = 1024
BATCH = 8
SEQ = 2048
DEPTH = 2
DEC_BATCH = 128
DEC_SEQ = 1
PAST_LEN = 16384
PAGE_SIZE = 128

N_MEM = 256
E_SSM = D_MODEL // 2
SSM_GROUP = 16
N_SSM_GROUPS = E_SSM // SSM_GROUP
SSM_STATE = 64
DT_MIN = 1e-3
DT_MAX = 1e-1
E_CONV = D_MODEL // 2
CONV_WIDTH = 31
E_XATT = D_MODEL // 2
N_XHEADS = 4
XHEAD_DIM = E_XATT // N_XHEADS
EPS = 1e-6

IN_SIZES = (E_SSM, E_SSM, 2 * E_CONV, E_CONV, E_XATT, E_XATT, D_MODEL, D_MODEL, D_MODEL)
IN_TOTAL = int(sum(IN_SIZES))
IN_SPLITS = [int(v) for v in np.cumsum(IN_SIZES)[:-1]]

kernel_name = "s5_conformer_memxattn_gated_hybrid_step"


def rmsnorm(x, g):
    xf = x.astype(jnp.float32)
    y = xf * lax.rsqrt(jnp.mean(xf * xf, axis=-1, keepdims=True) + EPS) * g.astype(jnp.float32)
    return y.astype(x.dtype)


def layernorm(x, g, b):
    xf = x.astype(jnp.float32)
    mu = jnp.mean(xf, axis=-1, keepdims=True)
    var = jnp.mean(jnp.square(xf - mu), axis=-1, keepdims=True)
    y = (xf - mu) * lax.rsqrt(var + EPS) * g.astype(jnp.float32) + b.astype(jnp.float32)
    return y.astype(x.dtype)


def _to_c(re, im):
    return lax.complex(re.astype(jnp.float32), im.astype(jnp.float32))


def _lin_rec(e1, e2):
    a1, b1 = e1
    a2, b2 = e2
    return a1 * a2, a2 * b1 + b2


def s5_branch(u, s0, a_re, a_im, log_dt, b_re, b_im, c_re, c_im, d):
    bsz, L, _ = u.shape
    uf = u.astype(jnp.float32).reshape(bsz, L, N_SSM_GROUPS, SSM_GROUP)
    lam = _to_c(a_re, a_im)
    dt = jnp.exp(log_dt.astype(jnp.float32))[:, None]
    lam_bar = jnp.exp(lam * dt)
    b_bar = ((lam_bar - 1.0) / lam)[..., None] * _to_c(b_re, b_im)
    bu = jnp.einsum('blgh,gph->blgp', uf.astype(jnp.complex64), b_bar)
    a = jnp.broadcast_to(lam_bar, bu.shape)
    a_cum, xs = lax.associative_scan(_lin_rec, (a, bu), axis=1)
    xs = xs + a_cum * s0[:, None]
    y = jnp.einsum('blgp,ghp->blgh', xs, _to_c(c_re, c_im)).real
    y = y + d.astype(jnp.float32).reshape(N_SSM_GROUPS, SSM_GROUP) * uf
    return y.reshape(bsz, L, E_SSM), xs[:, -1]


def causal_depthwise_conv(v, buf, w, b):
    full = jnp.concatenate([buf.astype(v.dtype), v], axis=1)
    out = lax.conv_general_dilated(full, w.astype(v.dtype)[:, None, :], window_strides=(1,), padding='VALID',
                                   dimension_numbers=('NWC', 'WIO', 'NWC'), feature_group_count=E_CONV)
    return out + b.astype(v.dtype), full[:, -(CONV_WIDTH - 1):]


def mem_kv(mem, g, w_k, w_v):
    hm = rmsnorm(mem, g)
    bsz = mem.shape[0]
    k = (hm @ w_k).reshape(bsz, N_MEM, N_XHEADS, XHEAD_DIM)
    v = (hm @ w_v).reshape(bsz, N_MEM, N_XHEADS, XHEAD_DIM)
    return k, v


def mixer_layer(x, s0_re, s0_im, conv_buf, mk, mv, p):
    bsz, L, _ = x.shape
    h = rmsnorm(x, p['norm_g'])
    u_a, z_a, u_b, z_b, q, z_x, g_a, g_b, g_x = jnp.split(h @ p['w_in'], IN_SPLITS, axis=-1)
    s0 = _to_c(s0_re, s0_im)
    y_a, s_new = s5_branch(u_a, s0, p['a_re'], p['a_im'], p['log_dt'], p['b_re'], p['b_im'],
                           p['c_re'], p['c_im'], p['d'])
    y_a = jax.nn.gelu(y_a).astype(x.dtype)
    y_a = y_a * jax.nn.sigmoid(y_a @ p['w_glu'] + p['b_glu'])
    o_a = (y_a * jax.nn.silu(z_a)) @ p['w_br_ssm']
    v_b = u_b[..., :E_CONV] * jax.nn.sigmoid(u_b[..., E_CONV:])
    c_b, buf_new = causal_depthwise_conv(v_b, conv_buf, p['conv_w'], p['conv_b'])
    c_b = jax.nn.silu(layernorm(c_b, p['ln_g'], p['ln_b']))
    o_b = (c_b * jax.nn.silu(z_b)) @ p['w_br_conv']
    qh = q.reshape(bsz, L, N_XHEADS, XHEAD_DIM).astype(jnp.float32)
    s = jnp.einsum('blhd,bmhd->bhlm', qh, mk.astype(jnp.float32)) * (1.0 / math.sqrt(XHEAD_DIM))
    prob = jax.nn.softmax(s, axis=-1).astype(mv.dtype)
    o_x = jnp.einsum('bhlm,bmhd->blhd', prob, mv).reshape(bsz, L, E_XATT).astype(x.dtype)
    o_x = (o_x * jax.nn.silu(z_x)) @ p['w_br_xatt']
    m = jax.nn.sigmoid(g_a) * o_a + jax.nn.sigmoid(g_b) * o_b + jax.nn.sigmoid(g_x) * o_x
    x = x + (m @ p['w_out']).astype(x.dtype)
    return x, jnp.real(s_new), jnp.imag(s_new), buf_new


def setup_inputs(seed: int = 0) -> dict:
    key = jax.random.key(seed)
    ks = jax.random.split(key, 40)
    f32 = jnp.float32
    nrm = lambda k, shape, scale: (jax.random.normal(k, shape, f32) * scale)
    G, P, H = N_SSM_GROUPS, SSM_STATE, SSM_GROUP
    inp = {}
    inp['x_prompt'] = nrm(ks[0], (BATCH, SEQ, D_MODEL), 1.0)
    inp['x_sample'] = nrm(ks[1], (DEC_BATCH, DEC_SEQ, D_MODEL), 1.0)
    inp['mem_prompt'] = nrm(ks[2], (BATCH, N_MEM, D_MODEL), 1.0)
    inp['state_ssm_re'] = nrm(ks[3], (DEPTH, DEC_BATCH, G, P), 0.1)
    inp['state_ssm_im'] = nrm(ks[4], (DEPTH, DEC_BATCH, G, P), 0.1)
    inp['state_conv'] = nrm(ks[5], (DEPTH, DEC_BATCH, CONV_WIDTH - 1, E_CONV), 1.0)
    inp['cache_mem_k'] = nrm(ks[6], (DEPTH, DEC_BATCH, N_MEM, N_XHEADS, XHEAD_DIM), 1.0)
    inp['cache_mem_v'] = nrm(ks[7], (DEPTH, DEC_BATCH, N_MEM, N_XHEADS, XHEAD_DIM), 1.0)
    inp['norm_g'] = 1.0 + nrm(ks[8], (DEPTH, D_MODEL), 0.02)
    inp['w_in'] = nrm(ks[9], (DEPTH, D_MODEL, IN_TOTAL), D_MODEL ** -0.5)
    inp['ssm_a_re'] = -0.5 + nrm(ks[10], (DEPTH, G, P), 0.01)
    inp['ssm_a_im'] = math.pi * jnp.broadcast_to(jnp.arange(P, dtype=f32), (DEPTH, G, P)) + nrm(ks[11], (DEPTH, G, P), 0.01)
    inp['ssm_log_dt'] = jax.random.uniform(ks[12], (DEPTH, G), f32, math.log(DT_MIN), math.log(DT_MAX))
    inp['ssm_b_re'] = nrm(ks[13], (DEPTH, G, P, H), (2 * H) ** -0.5)
    inp['ssm_b_im'] = nrm(ks[14], (DEPTH, G, P, H), (2 * H) ** -0.5)
    inp['ssm_c_re'] = nrm(ks[15], (DEPTH, G, H, P), (2 * P) ** -0.5 * 4.0)
    inp['ssm_c_im'] = nrm(ks[16], (DEPTH, G, H, P), (2 * P) ** -0.5 * 4.0)
    inp['ssm_d'] = nrm(ks[17], (DEPTH, E_SSM), 1.0)
    inp['w_glu'] = nrm(ks[18], (DEPTH, E_SSM, E_SSM), E_SSM ** -0.5)
    inp['b_glu'] = nrm(ks[19], (DEPTH, E_SSM), 0.01)
    inp['w_br_ssm'] = nrm(ks[20], (DEPTH, E_SSM, D_MODEL), E_SSM ** -0.5)
    inp['conv_w'] = nrm(ks[21], (DEPTH, CONV_WIDTH, E_CONV), CONV_WIDTH ** -0.5)
    inp['conv_b'] = nrm(ks[22], (DEPTH, E_CONV), 0.01)
    inp['conv_ln_g'] = 1.0 + nrm(ks[23], (DEPTH, E_CONV), 0.02)
    inp['conv_ln_b'] = nrm(ks[24], (DEPTH, E_CONV), 0.01)
    inp['w_br_conv'] = nrm(ks[25], (DEPTH, E_CONV, D_MODEL), E_CONV ** -0.5)
    inp['mem_norm_g'] = 1.0 + nrm(ks[26], (DEPTH, D_MODEL), 0.02)
    inp['w_k'] = nrm(ks[27], (DEPTH, D_MODEL, E_XATT), D_MODEL ** -0.5)
    inp['w_v'] = nrm(ks[28], (DEPTH, D_MODEL, E_XATT), D_MODEL ** -0.5)
    inp['w_br_xatt'] = nrm(ks[29], (DEPTH, E_XATT, D_MODEL), E_XATT ** -0.5)
    inp['w_out'] = nrm(ks[30], (DEPTH, D_MODEL, D_MODEL), D_MODEL ** -0.5)
    inp['final_norm_g'] = 1.0 + nrm(ks[31], (D_MODEL,), 0.02)
    return inp


def reference(x_prompt, x_sample, mem_prompt, state_ssm_re, state_ssm_im, state_conv, cache_mem_k, cache_mem_v,
              norm_g, w_in, ssm_a_re, ssm_a_im, ssm_log_dt, ssm_b_re, ssm_b_im, ssm_c_re, ssm_c_im, ssm_d,
              w_glu, b_glu, w_br_ssm, conv_w, conv_b, conv_ln_g, conv_ln_b, w_br_conv,
              mem_norm_g, w_k, w_v, w_br_xatt, w_out, final_norm_g):
    bp = x_prompt.shape[0]
    xp, xs = x_prompt, x_sample
    p_re, p_im, p_conv, p_k, p_v = [], [], [], [], []
    s_re, s_im, s_conv = [], [], []
    for l in range(DEPTH):
        prm = {'norm_g': norm_g[l], 'w_in': w_in[l], 'a_re': ssm_a_re[l], 'a_im': ssm_a_im[l],
               'log_dt': ssm_log_dt[l], 'b_re': ssm_b_re[l], 'b_im': ssm_b_im[l], 'c_re': ssm_c_re[l],
               'c_im': ssm_c_im[l], 'd': ssm_d[l], 'w_glu': w_glu[l], 'b_glu': b_glu[l],
               'w_br_ssm': w_br_ssm[l], 'conv_w': conv_w[l], 'conv_b': conv_b[l], 'ln_g': conv_ln_g[l],
               'ln_b': conv_ln_b[l], 'w_br_conv': w_br_conv[l], 'w_br_xatt': w_br_xatt[l], 'w_out': w_out[l]}
        mk, mv = mem_kv(mem_prompt, mem_norm_g[l], w_k[l], w_v[l])
        z_state = jnp.zeros((bp, N_SSM_GROUPS, SSM_STATE), jnp.float32)
        z_buf = jnp.zeros((bp, CONV_WIDTH - 1, E_CONV), xp.dtype)
        xp, r, i, b = mixer_layer(xp, z_state, z_state, z_buf, mk, mv, prm)
        p_re.append(r); p_im.append(i); p_conv.append(b); p_k.append(mk); p_v.append(mv)
        xs, r, i, b = mixer_layer(xs, state_ssm_re[l], state_ssm_im[l], state_conv[l],
                                  cache_mem_k[l], cache_mem_v[l], prm)
        s_re.append(r); s_im.append(i); s_conv.append(b)
    y_prompt = rmsnorm(xp, final_norm_g)
    y_sample = rmsnorm(xs, final_norm_g)
    return (y_prompt, y_sample, jnp.stack(p_re), jnp.stack(p_im), jnp.stack(p_conv), jnp.stack(p_k), jnp.stack(p_v),
            jnp.stack(s_re), jnp.stack(s_im), jnp.stack(s_conv))
```

```python
import functools
import math

import jax
import jax.numpy as jnp
from jax import lax
from jax.experimental import pallas as pl
from jax.experimental.pallas import tpu as pltpu

D_MODEL = 1024
E_BR = 512
N_GROUPS = 32
GROUP = 16
N_STATE = 64
N_SSM = N_GROUPS * N_STATE
CONV_K = 31
N_HEADS = 4
HEAD_DIM = 128
N_MEM = 256
EPS = 1e-6
DEPTH = 2

LANES = 128
SUBLANES = 8
N_BLK = E_BR // LANES
BLK_STATE = N_SSM // N_BLK

C_UA, C_ZA, C_UB, C_ZB, C_Q, C_ZX, C_GA, C_GB, C_GX, C_END = (
    0, 512, 1024, 2048, 2560, 3072, 3584, 4608, 5632, 6656)

VMEM_LIMIT = 60000 * 1024

_F32 = jnp.float32
_BF16 = jnp.bfloat16


def _dot(a, b):
    return jnp.dot(a, b, preferred_element_type=_F32)


def _dot_exact(a, b):
    return jnp.dot(a, b, preferred_element_type=_F32, precision=lax.Precision.HIGHEST)


def _sigmoid(x):
    return jax.nn.sigmoid(x)


def _silu(x):
    return x * _sigmoid(x)


def _rmsnorm(x, g):
    return x * lax.rsqrt(jnp.mean(x * x, axis=-1, keepdims=True) + EPS) * g


def _layernorm(x, g, b):
    mu = jnp.mean(x, axis=-1, keepdims=True)
    xc = x - mu
    var = jnp.mean(xc * xc, axis=-1, keepdims=True)
    return xc * lax.rsqrt(var + EPS) * g + b


def _const_spec(shape):
    nd = len(shape)
    return pl.BlockSpec(shape, lambda i, _nd=nd: (0,) * _nd, pipeline_mode=pl.Buffered(1))


def _memkv_kernel(mem_ref, g_ref, wkv_ref, k_ref, v_ref, kb_ref, vb_ref):
    hm = _rmsnorm(mem_ref[...], g_ref[...]).astype(_BF16)
    kv = _dot(hm, wkv_ref[...])
    k = kv[:, :E_BR]
    v = kv[:, E_BR:]
    k_ref[...] = k
    v_ref[...] = v
    kb_ref[...] = k.astype(_BF16)
    vb_ref[...] = v.astype(_BF16)


def _memkv(mem2d, g, wkv):
    rows = mem2d.shape[0]
    blk = 512
    out_f = jax.ShapeDtypeStruct((rows, E_BR), _F32)
    out_b = jax.ShapeDtypeStruct((rows, E_BR), _BF16)
    row_spec = pl.BlockSpec((blk, E_BR), lambda i: (i, 0))
    return pl.pallas_call(
        _memkv_kernel,
        grid=(rows // blk,),
        in_specs=[pl.BlockSpec((blk, D_MODEL), lambda i: (i, 0)),
                  _const_spec((1, D_MODEL)),
                  _const_spec((D_MODEL, 2 * E_BR))],
        out_specs=[row_spec, row_spec, row_spec, row_spec],
        out_shape=[out_f, out_f, out_b, out_b],
        compiler_params=pltpu.CompilerParams(dimension_semantics=("arbitrary",)),
        name="mem_kv",
    )(mem2d, g, wkv)


def _ssm_readout(xs_bf, ua, cmat_ref, d_ref, wglu_ref, bglu_ref):
    ys = [_dot(xs_bf[:, 2 * BLK_STATE * j:2 * BLK_STATE * (j + 1)], cmat_ref[j]) for j in range(N_BLK)]
    y = jnp.concatenate(ys, axis=-1) + d_ref[...] * ua
    y = jax.nn.gelu(y, approximate=True)
    return y * _sigmoid(_dot(y.astype(_BF16), wglu_ref[...]) + bglu_ref[...])


def _prompt_kernel(x_ref, ng_ref, win_ref, lre_ref, lim_ref, bmat_ref, cmat_ref, d_ref, wglu_ref, bglu_ref,
                   wbra_ref, cw_ref, cb_ref, lng_ref, lnb_ref, wbrb_ref, k_ref, v_ref, wbrx_ref, wout_ref,
                   fg_ref,
                   y_ref, sre_ref, sim_ref, tail_ref,
                   h_s, ua_s, bu_s, full_s, q_s, o_s,
                   *, rows, chunk, n_seq, final):
    i = pl.program_id(0)
    n_t = rows // n_seq
    tail_rows = (CONV_K - 1) * n_seq
    n_chunks = rows // chunk
    scale = 1.0 / math.sqrt(HEAD_DIM)

    @pl.when(i == 0)
    def _init():
        sre_ref[...] = jnp.zeros_like(sre_ref)
        sim_ref[...] = jnp.zeros_like(sim_ref)
        full_s[0:tail_rows, :] = jnp.zeros((tail_rows, E_BR), _F32)

    def phase1(c, carry):
        r0 = pl.multiple_of(c * chunk, chunk)
        h = _rmsnorm(x_ref[pl.ds(r0, chunk), :], ng_ref[...]).astype(_BF16)
        h_s[pl.ds(r0, chunk), :] = h
        ua = _dot(h, win_ref[:, C_UA:C_ZA])
        ua_s[pl.ds(r0, chunk), :] = ua
        ua_bf = ua.astype(_BF16)
        for j in range(N_BLK):
            bu_s[pl.ds(r0, chunk), 2 * BLK_STATE * j:2 * BLK_STATE * (j + 1)] = _dot(
                ua_bf[:, LANES * j:LANES * (j + 1)], bmat_ref[j])
        ub = _dot(h, win_ref[:, C_UB:C_ZB])
        full_s[pl.ds(tail_rows + r0, chunk), :] = ub[:, :E_BR] * _sigmoid(ub[:, E_BR:])
        q = _dot(h, win_ref[:, C_Q:C_ZX]) * scale
        for hh in range(N_HEADS):
            q_s[hh, pl.ds(r0, chunk), :] = q[:, HEAD_DIM * hh:HEAD_DIM * (hh + 1)]
        return carry

    lax.fori_loop(0, n_chunks, phase1, 0)

    for j in range(N_BLK):
        sl_s = slice(BLK_STATE * j, BLK_STATE * (j + 1))
        sl_re = slice(2 * BLK_STATE * j, 2 * BLK_STATE * j + BLK_STATE)
        sl_im = slice(2 * BLK_STATE * j + BLK_STATE, 2 * BLK_STATE * (j + 1))
        lam_re = lre_ref[:, sl_s]
        lam_im = lim_ref[:, sl_s]

        def step(t, carry, sl_re=sl_re, sl_im=sl_im, lam_re=lam_re, lam_im=lam_im):
            s_re, s_im = carry
            r = pl.multiple_of(t * n_seq, n_seq)
            n_re = lam_re * s_re - lam_im * s_im + bu_s[pl.ds(r, n_seq), sl_re]
            n_im = lam_re * s_im + lam_im * s_re + bu_s[pl.ds(r, n_seq), sl_im]
            bu_s[pl.ds(r, n_seq), sl_re] = n_re
            bu_s[pl.ds(r, n_seq), sl_im] = n_im
            return n_re, n_im

        s_re, s_im = lax.fori_loop(0, n_t, step, (sre_ref[:, sl_s], sim_ref[:, sl_s]), unroll=4)
        sre_ref[:, sl_s] = s_re
        sim_ref[:, sl_s] = s_im

    for b in range(n_seq):
        for hh in range(N_HEADS):
            hs = slice(HEAD_DIM * hh, HEAD_DIM * (hh + 1))
            qb = q_s[hh, pl.ds(b, n_t, stride=n_seq), :].astype(_BF16)
            s = lax.dot_general(qb, k_ref[b, :, hs], (((1,), (1,)), ((), ())), preferred_element_type=_F32)
            e = jnp.exp(s - jnp.max(s, axis=-1, keepdims=True))
            den = jnp.sum(e, axis=-1, keepdims=True)
            o = _dot(e.astype(_BF16), v_ref[b, :, hs]) / den
            o_s[hh, pl.ds(b, n_t, stride=n_seq), :] = o

    def phase3(c, carry):
        r0 = pl.multiple_of(c * chunk, chunk)
        rs = pl.ds(r0, chunk)
        h = h_s[rs, :]
        ya = _ssm_readout(bu_s[rs, :].astype(_BF16), ua_s[rs, :], cmat_ref, d_ref, wglu_ref, bglu_ref)
        ya = ya * _silu(_dot(h, win_ref[:, C_ZA:C_UB]))
        m = _sigmoid(_dot(h, win_ref[:, C_GA:C_GB])) * _dot(ya.astype(_BF16), wbra_ref[...])
        acc = jnp.zeros((chunk, E_BR), _F32) + cb_ref[...]
        for k in range(CONV_K):
            acc = acc + cw_ref[k:k + 1, :] * full_s[pl.ds(r0 + k * n_seq, chunk), :]
        cb = _silu(_layernorm(acc, lng_ref[...], lnb_ref[...]))
        cb = cb * _silu(_dot(h, win_ref[:, C_ZB:C_Q]))
        m = m + _sigmoid(_dot(h, win_ref[:, C_GB:C_GX])) * _dot(cb.astype(_BF16), wbrb_ref[...])
        ox = jnp.concatenate([o_s[hh, rs, :] for hh in range(N_HEADS)], axis=-1)
        ox = ox * _silu(_dot(h, win_ref[:, C_ZX:C_GA]))
        m = m + _sigmoid(_dot(h, win_ref[:, C_GX:C_END])) * _dot(ox.astype(_BF16), wbrx_ref[...])
        out = x_ref[rs, :] + _dot(m.astype(_BF16), wout_ref[...])
        if final:
            out = _rmsnorm(out, fg_ref[...])
        y_ref[rs, :] = out
        return carry

    lax.fori_loop(0, n_chunks, phase3, 0)

    new_tail = full_s[rows:rows + tail_rows, :]
    full_s[0:tail_rows, :] = new_tail

    @pl.when(i == pl.num_programs(0) - 1)
    def _fin():
        tail_ref[...] = new_tail


def _prompt_layer(x2d, p, kb, vb, final_g, *, n_seq, rows, chunk, final):
    total = x2d.shape[0]
    tail_rows = (CONV_K - 1) * n_seq
    kernel = functools.partial(_prompt_kernel, rows=rows, chunk=chunk, n_seq=n_seq, final=final)
    row_spec = pl.BlockSpec((rows, D_MODEL), lambda i: (i, 0))
    ins = [x2d, p['norm_g'], p['w_in_bf'], p['lam_re8'], p['lam_im8'], p['bmat_bf'], p['cmat_bf'], p['d'],
           p['w_glu_bf'], p['b_glu'], p['w_br_ssm_bf'], p['conv_w'], p['conv_b'], p['ln_g'], p['ln_b'],
           p['w_br_conv_bf'], kb, vb, p['w_br_xatt_bf'], p['w_out_bf'], final_g]
    in_specs = [row_spec] + [_const_spec(a.shape) for a in ins[1:]]
    state_spec = pl.BlockSpec((n_seq, N_SSM), lambda i: (0, 0))
    return pl.pallas_call(
        kernel,
        grid=(total // rows,),
        in_specs=in_specs,
        out_specs=[row_spec, state_spec, state_spec, pl.BlockSpec((tail_rows, E_BR), lambda i: (0, 0))],
        out_shape=[jax.ShapeDtypeStruct((total, D_MODEL), _F32),
                   jax.ShapeDtypeStruct((n_seq, N_SSM), _F32),
                   jax.ShapeDtypeStruct((n_seq, N_SSM), _F32),
                   jax.ShapeDtypeStruct((tail_rows, E_BR), _F32)],
        scratch_shapes=[pltpu.VMEM((rows, D_MODEL), _BF16),
                        pltpu.VMEM((rows, E_BR), _F32),
                        pltpu.VMEM((rows, 2 * N_SSM), _F32),
                        pltpu.VMEM((tail_rows + rows, E_BR), _F32),
                        pltpu.VMEM((N_HEADS, rows, HEAD_DIM), _F32),
                        pltpu.VMEM((N_HEADS, rows, HEAD_DIM), _F32)],
        compiler_params=pltpu.CompilerParams(dimension_semantics=("arbitrary",), vmem_limit_bytes=VMEM_LIMIT),
        name="prompt_layer",
    )(*ins)


def _sample_kernel(x_ref, ng_ref, win_ref, wua_ref, wub_ref, lre_ref, lim_ref, bmat_ref, cmat_ref, d_ref,
                   wglu_ref, bglu_ref, wbra_ref, cw_ref, cb_ref, lng_ref, lnb_ref, wbrb_ref,
                   s0re_ref, s0im_ref, cst_ref, k_ref, v_ref, hsel_ref, wbrx_ref, wout_ref, fg_ref,
                   y_ref, sre_ref, sim_ref, vnew_ref,
                   h_s, m_s, q_s, o_s,
                   *, blk, final):
    i = pl.program_id(0)
    scale = 1.0 / math.sqrt(HEAD_DIM)

    @pl.when(i == 0)
    def _pre():
        x = x_ref[...]
        hf = _rmsnorm(x, ng_ref[...])
        h = hf.astype(_BF16)
        h_s[...] = h
        ua = _dot_exact(hf, wua_ref[...])
        bu = jnp.concatenate([_dot_exact(ua[:, LANES * j:LANES * (j + 1)], bmat_ref[j]) for j in range(N_BLK)],
                             axis=-1)
        xs_parts = []
        for j in range(N_BLK):
            sl_s = slice(BLK_STATE * j, BLK_STATE * (j + 1))
            b_re = bu[:, 2 * BLK_STATE * j:2 * BLK_STATE * j + BLK_STATE]
            b_im = bu[:, 2 * BLK_STATE * j + BLK_STATE:2 * BLK_STATE * (j + 1)]
            l_re, l_im = lre_ref[:, sl_s], lim_ref[:, sl_s]
            o_re, o_im = s0re_ref[:, sl_s], s0im_ref[:, sl_s]
            n_re = l_re * o_re - l_im * o_im + b_re
            n_im = l_re * o_im + l_im * o_re + b_im
            sre_ref[:, sl_s] = n_re
            sim_ref[:, sl_s] = n_im
            xs_parts += [n_re, n_im]
        xs_bf = jnp.concatenate(xs_parts, axis=-1).astype(_BF16)
        ya = _ssm_readout(xs_bf, ua, cmat_ref, d_ref, wglu_ref, bglu_ref)
        ya = ya * _silu(_dot(h, win_ref[:, C_ZA:C_UB]))
        m = _sigmoid(_dot(h, win_ref[:, C_GA:C_GB])) * _dot(ya.astype(_BF16), wbra_ref[...])
        ub = _dot_exact(hf, wub_ref[...])
        vb = ub[:, :E_BR] * _sigmoid(ub[:, E_BR:])
        acc = cb_ref[...] + cw_ref[CONV_K - 1:CONV_K, :] * vb
        for k in range(CONV_K - 1):
            acc = acc + cw_ref[k:k + 1, :] * cst_ref[:, E_BR * k:E_BR * (k + 1)]
        vnew_ref[...] = vb
        cb = _silu(_layernorm(acc, lng_ref[...], lnb_ref[...]))
        cb = cb * _silu(_dot(h, win_ref[:, C_ZB:C_Q]))
        m = m + _sigmoid(_dot(h, win_ref[:, C_GB:C_GX])) * _dot(cb.astype(_BF16), wbrb_ref[...])
        m_s[...] = m
        q_s[...] = _dot(h, win_ref[:, C_Q:C_ZX]) * scale

    for s in range(blk):
        row = pl.ds(i * blk + s, 1)
        kq = (k_ref[s] * q_s[row, :]).astype(_BF16)
        sc = _dot(kq, hsel_ref[...])
        e = jnp.exp(sc - jnp.max(sc, axis=0, keepdims=True))
        den = jnp.sum(e, axis=0, keepdims=True)
        num = jnp.sum(e * v_ref[s], axis=0, keepdims=True)
        o_s[row, :] = num / den

    @pl.when(i == pl.num_programs(0) - 1)
    def _post():
        h = h_s[...]
        ox = o_s[...] * _silu(_dot(h, win_ref[:, C_ZX:C_GA]))
        m = m_s[...] + _sigmoid(_dot(h, win_ref[:, C_GX:C_END])) * _dot(ox.astype(_BF16), wbrx_ref[...])
        out = x_ref[...] + _dot(m.astype(_BF16), wout_ref[...])
        if final:
            out = _rmsnorm(out, fg_ref[...])
        y_ref[...] = out


def _sample_layer(x2d, p, w_in_f32, layer, s0re, s0im, cst, kc, vc, hsel, final_g, *, blk, final):
    n = x2d.shape[0]
    kernel = functools.partial(_sample_kernel, blk=blk, final=final)
    kv_spec = pl.BlockSpec((blk, N_MEM, E_BR), lambda i: (i, 0, 0))
    ins = [x2d, p['norm_g'], p['w_in_bf'], w_in_f32, w_in_f32, p['lam_re1'], p['lam_im1'], p['bmat'], p['cmat_bf'],
           p['d'], p['w_glu_bf'], p['b_glu'], p['w_br_ssm_bf'], p['conv_w'], p['conv_b'], p['ln_g'], p['ln_b'],
           p['w_br_conv_bf'], s0re, s0im, cst, kc, vc, hsel, p['w_br_xatt_bf'], p['w_out_bf'], final_g]
    in_specs = [_const_spec(a.shape) for a in ins]
    in_specs[3] = pl.BlockSpec((None, D_MODEL, E_BR), lambda i, _l=layer: (_l, 0, C_UA // E_BR),
                               pipeline_mode=pl.Buffered(1))
    in_specs[4] = pl.BlockSpec((None, D_MODEL, 2 * E_BR), lambda i, _l=layer: (_l, 0, C_UB // (2 * E_BR)),
                               pipeline_mode=pl.Buffered(1))
    in_specs[21] = kv_spec
    in_specs[22] = kv_spec
    full = lambda shape: pl.BlockSpec(shape, lambda i: (0, 0))
    return pl.pallas_call(
        kernel,
        grid=(n // blk,),
        in_specs=in_specs,
        out_specs=[full((n, D_MODEL)), full((n, N_SSM)), full((n, N_SSM)), full((n, E_BR))],
        out_shape=[jax.ShapeDtypeStruct((n, D_MODEL), _F32),
                   jax.ShapeDtypeStruct((n, N_SSM), _F32),
                   jax.ShapeDtypeStruct((n, N_SSM), _F32),
                   jax.ShapeDtypeStruct((n, E_BR), _F32)],
        scratch_shapes=[pltpu.VMEM((n, D_MODEL), _BF16),
                        pltpu.VMEM((n, D_MODEL), _F32),
                        pltpu.VMEM((n, E_BR), _F32),
                        pltpu.VMEM((n, E_BR), _F32)],
        compiler_params=pltpu.CompilerParams(dimension_semantics=("arbitrary",), vmem_limit_bytes=VMEM_LIMIT),
        name="sample_layer",
    )(*ins)


def _prep_layer(l, norm_g, w_in, a_re, a_im, log_dt, b_re, b_im, c_re, c_im, d, w_glu, b_glu, w_br_ssm,
                conv_w, conv_b, ln_g, ln_b, w_br_conv, w_br_xatt, w_out):
    dt = jnp.exp(log_dt[l])[:, None]
    ar, ai = a_re[l], a_im[l]
    mag = jnp.exp(ar * dt)
    lr = mag * jnp.cos(ai * dt)
    li = mag * jnp.sin(ai * dt)
    den = ar * ar + ai * ai
    cr = ((lr - 1.0) * ar + li * ai) / den
    ci = (li * ar - (lr - 1.0) * ai) / den
    bb_re = cr[..., None] * b_re[l] - ci[..., None] * b_im[l]
    bb_im = cr[..., None] * b_im[l] + ci[..., None] * b_re[l]
    eye = jnp.eye(N_GROUPS // N_BLK, dtype=_F32)

    def b_blocks(bb):
        bb = bb.reshape(N_BLK, N_GROUPS // N_BLK, N_STATE, GROUP)
        return jnp.einsum('jgph,gk->jghkp', bb, eye).reshape(N_BLK, LANES, BLK_STATE)

    def c_blocks(cc):
        cc = cc.reshape(N_BLK, N_GROUPS // N_BLK, GROUP, N_STATE)
        return jnp.einsum('jghp,gk->jkpgh', cc, eye).reshape(N_BLK, BLK_STATE, LANES)

    bmat = jnp.concatenate([b_blocks(bb_re), b_blocks(bb_im)], axis=-1)
    cmat = jnp.concatenate([c_blocks(c_re[l]), -c_blocks(c_im[l])], axis=1)
    lr1 = lr.reshape(1, N_SSM)
    li1 = li.reshape(1, N_SSM)
    row = lambda v: v.reshape(1, -1)
    return {
        'norm_g': row(norm_g[l]), 'w_in_bf': w_in[l].astype(_BF16),
        'lam_re1': lr1, 'lam_im1': li1,
        'lam_re8': jnp.broadcast_to(lr1, (SUBLANES, N_SSM)), 'lam_im8': jnp.broadcast_to(li1, (SUBLANES, N_SSM)),
        'bmat': bmat, 'bmat_bf': bmat.astype(_BF16), 'cmat_bf': cmat.astype(_BF16),
        'd': row(d[l]), 'w_glu_bf': w_glu[l].astype(_BF16), 'b_glu': row(b_glu[l]),
        'w_br_ssm_bf': w_br_ssm[l].astype(_BF16), 'conv_w': conv_w[l], 'conv_b': row(conv_b[l]),
        'ln_g': row(ln_g[l]), 'ln_b': row(ln_b[l]), 'w_br_conv_bf': w_br_conv[l].astype(_BF16),
        'w_br_xatt_bf': w_br_xatt[l].astype(_BF16), 'w_out_bf': w_out[l].astype(_BF16),
    }


def kernel(x_prompt, x_sample, mem_prompt, state_ssm_re, state_ssm_im, state_conv, cache_mem_k, cache_mem_v,
           norm_g, w_in, ssm_a_re, ssm_a_im, ssm_log_dt, ssm_b_re, ssm_b_im, ssm_c_re, ssm_c_im, ssm_d,
           w_glu, b_glu, w_br_ssm, conv_w, conv_b, conv_ln_g, conv_ln_b, w_br_conv,
           mem_norm_g, w_k, w_v, w_br_xatt, w_out, final_norm_g):
    bp, seq, _ = x_prompt.shape
    bs = x_sample.shape[0]
    assert bp == SUBLANES and x_sample.shape[1] == 1
    rows, chunk, sample_blk = 512, 256, 4
    hist = (CONV_K - 1) * E_BR

    xp = x_prompt.transpose(1, 0, 2).reshape(seq * bp, D_MODEL)
    xs = x_sample.reshape(bs, D_MODEL)
    mem2d = mem_prompt.reshape(bp * N_MEM, D_MODEL)
    final_g = final_norm_g.reshape(1, D_MODEL)
    lane_head = jnp.arange(E_BR, dtype=jnp.int32) // HEAD_DIM
    hsel = (lane_head[:, None] == lane_head[None, :]).astype(_BF16)

    p_re, p_im, p_conv, p_k, p_v, s_re, s_im, s_conv = [], [], [], [], [], [], [], []
    for l in range(DEPTH):
        p = _prep_layer(l, norm_g, w_in, ssm_a_re, ssm_a_im, ssm_log_dt, ssm_b_re, ssm_b_im, ssm_c_re, ssm_c_im,
                        ssm_d, w_glu, b_glu, w_br_ssm, conv_w, conv_b, conv_ln_g, conv_ln_b, w_br_conv,
                        w_br_xatt, w_out)
        final = l == DEPTH - 1
        wkv = jnp.concatenate([w_k[l], w_v[l]], axis=-1).astype(_BF16)
        k, v, kb, vb = _memkv(mem2d, mem_norm_g[l].reshape(1, D_MODEL), wkv)
        xp, re, im, tail = _prompt_layer(xp, p, kb.reshape(bp, N_MEM, E_BR), vb.reshape(bp, N_MEM, E_BR), final_g,
                                         n_seq=bp, rows=rows, chunk=chunk, final=final)
        p_re.append(re.reshape(bp, N_GROUPS, N_STATE))
        p_im.append(im.reshape(bp, N_GROUPS, N_STATE))
        p_conv.append(tail.reshape(CONV_K - 1, bp, E_BR).transpose(1, 0, 2))
        p_k.append(k.reshape(bp, N_MEM, N_HEADS, HEAD_DIM))
        p_v.append(v.reshape(bp, N_MEM, N_HEADS, HEAD_DIM))

        xs, re, im, vnew = _sample_layer(
            xs, p, w_in, l, state_ssm_re[l].reshape(bs, N_SSM), state_ssm_im[l].reshape(bs, N_SSM),
            state_conv[l].reshape(bs, hist), cache_mem_k[l].reshape(bs, N_MEM, E_BR),
            cache_mem_v[l].reshape(bs, N_MEM, E_BR), hsel, final_g, blk=sample_blk, final=final)
        s_re.append(re.reshape(bs, N_GROUPS, N_STATE))
        s_im.append(im.reshape(bs, N_GROUPS, N_STATE))
        s_conv.append(jnp.concatenate([state_conv[l][:, 1:, :], vnew[:, None, :]], axis=1))

    y_prompt = xp.reshape(seq, bp, D_MODEL).transpose(1, 0, 2)
    y_sample = xs.reshape(bs, 1, D_MODEL)
    return (y_prompt, y_sample, jnp.stack(p_re), jnp.stack(p_im), jnp.stack(p_conv), jnp.stack(p_k),
            jnp.stack(p_v), jnp.stack(s_re), jnp.stack(s_im), jnp.stack(s_conv))
```

```python
import functools
import math

import jax
import jax.numpy as jnp
from jax import lax
from jax.experimental import pallas as pl
from jax.experimental.pallas import tpu as pltpu

D_MODEL = 1024
E_BR = 512
N_GROUPS = 32
GROUP = 16
N_STATE = 64
N_SSM = N_GROUPS * N_STATE
CONV_K = 31
N_HEADS = 4
HEAD_DIM = 128
N_MEM = 256
EPS = 1e-6
DEPTH = 2

LANES = 128
SUBLANES = 8
N_BLK = E_BR // LANES
BLK_STATE = N_SSM // N_BLK
KV_ROWS = N_MEM * N_HEADS

C_UA, C_ZA, C_UB, C_ZB, C_Q, C_ZX, C_GA, C_GB, C_GX, C_END = (
    0, 512, 1024, 2048, 2560, 3072, 3584, 4608, 5632, 6656)

VMEM_LIMIT = 60000 * 1024

_F32 = jnp.float32
_BF16 = jnp.bfloat16


def _dot(a, b):
    return jnp.dot(a, b, preferred_element_type=_F32)


def _dot_exact(a, b):
    return jnp.dot(a, b, preferred_element_type=_F32, precision=lax.Precision.HIGHEST)


def _sigmoid(x):
    return jax.nn.sigmoid(x)


def _silu(x):
    return x * _sigmoid(x)


def _rmsnorm(x, g):
    return x * lax.rsqrt(jnp.mean(x * x, axis=-1, keepdims=True) + EPS) * g


def _layernorm(x, g, b):
    mu = jnp.mean(x, axis=-1, keepdims=True)
    xc = x - mu
    var = jnp.mean(xc * xc, axis=-1, keepdims=True)
    return xc * lax.rsqrt(var + EPS) * g + b


def _layer_spec(arr, layer):
    nd = arr.ndim - 1
    return pl.BlockSpec((None,) + arr.shape[1:], lambda *_, _l=layer, _nd=nd: (_l,) + (0,) * _nd,
                        pipeline_mode=pl.Buffered(1))


def _const_spec(arr):
    nd = arr.ndim
    return pl.BlockSpec(arr.shape, lambda *_, _nd=nd: (0,) * _nd, pipeline_mode=pl.Buffered(1))


def _memkv_kernel(mem_ref, g_ref, wkv_ref, k_ref, v_ref, kb_ref, va_ref):
    hm = _rmsnorm(mem_ref[...], g_ref[...]).astype(_BF16)
    kv = _dot(hm, wkv_ref[...])
    kb_ref[...] = kv[:, :E_BR].astype(_BF16)
    ones = jnp.ones((N_MEM, HEAD_DIM), _BF16)
    for hh in range(N_HEADS):
        kh = kv[:, HEAD_DIM * hh:HEAD_DIM * (hh + 1)]
        vh = kv[:, E_BR + HEAD_DIM * hh:E_BR + HEAD_DIM * (hh + 1)]
        k_ref[pl.ds(hh, N_MEM, stride=N_HEADS), :] = kh
        v_ref[pl.ds(hh, N_MEM, stride=N_HEADS), :] = vh
        va_ref[hh, :, 0:HEAD_DIM] = vh.astype(_BF16)
        va_ref[hh, :, HEAD_DIM:2 * HEAD_DIM] = ones


def _memkv(mem, g, wkv):
    depth, bp = g.shape[0], mem.shape[0]
    kv_f = jax.ShapeDtypeStruct((depth, bp, KV_ROWS, HEAD_DIM), _F32)
    kv_spec = pl.BlockSpec((None, None, KV_ROWS, HEAD_DIM), lambda l, b: (l, b, 0, 0))
    return pl.pallas_call(
        _memkv_kernel,
        grid=(depth, bp),
        in_specs=[pl.BlockSpec((None, N_MEM, D_MODEL), lambda l, b: (b, 0, 0)),
                  pl.BlockSpec((None, 1, D_MODEL), lambda l, b: (l, 0, 0)),
                  pl.BlockSpec((None, D_MODEL, 2 * E_BR), lambda l, b: (l, 0, 0))],
        out_specs=[kv_spec, kv_spec,
                   pl.BlockSpec((None, None, N_MEM, E_BR), lambda l, b: (l, b, 0, 0)),
                   pl.BlockSpec((None, None, N_HEADS, N_MEM, 2 * HEAD_DIM), lambda l, b: (l, b, 0, 0, 0))],
        out_shape=[kv_f, kv_f,
                   jax.ShapeDtypeStruct((depth, bp, N_MEM, E_BR), _BF16),
                   jax.ShapeDtypeStruct((depth, bp, N_HEADS, N_MEM, 2 * HEAD_DIM), _BF16)],
        compiler_params=pltpu.CompilerParams(dimension_semantics=("arbitrary", "arbitrary")),
        name="mem_kv",
    )(mem, g, wkv)


def _ssm_readout(xs_bf, ua, cmat_ref, d_ref, wglu_ref, bglu_ref):
    ys = [_dot(xs_bf[:, 2 * BLK_STATE * j:2 * BLK_STATE * (j + 1)], cmat_ref[j]) for j in range(N_BLK)]
    y = jnp.concatenate(ys, axis=-1) + d_ref[...] * ua
    y = jax.nn.gelu(y, approximate=True)
    return y * _sigmoid(_dot(y.astype(_BF16), wglu_ref[...]) + bglu_ref[...])


def _prompt_kernel(x_ref, ng_ref, win_ref, lre_ref, lim_ref, bmat_ref, cmat_ref, d_ref, wglu_ref, bglu_ref,
                   wbra_ref, cw_ref, cb_ref, lng_ref, lnb_ref, wbrb_ref, k_ref, va_ref, wbrx_ref, wout_ref,
                   fg_ref,
                   y_ref, sre_ref, sim_ref, tail_ref,
                   h_s, ua_s, bu_s, full_s, q_s, o_s,
                   *, rows, chunk, n_seq, final):
    i = pl.program_id(0)
    n_t = rows // n_seq
    tail_rows = (CONV_K - 1) * n_seq
    n_chunks = rows // chunk
    scale = 1.0 / math.sqrt(HEAD_DIM)

    @pl.when(i == 0)
    def _init():
        sre_ref[...] = jnp.zeros_like(sre_ref)
        sim_ref[...] = jnp.zeros_like(sim_ref)
        full_s[0:tail_rows, :] = jnp.zeros((tail_rows, E_BR), _F32)

    def phase1(c, carry):
        r0 = pl.multiple_of(c * chunk, chunk)
        h = _rmsnorm(x_ref[pl.ds(r0, chunk), :], ng_ref[...]).astype(_BF16)
        h_s[pl.ds(r0, chunk), :] = h
        ua = _dot(h, win_ref[:, C_UA:C_ZA])
        ua_s[pl.ds(r0, chunk), :] = ua
        ua_bf = ua.astype(_BF16)
        for j in range(N_BLK):
            bu_s[pl.ds(r0, chunk), 2 * BLK_STATE * j:2 * BLK_STATE * (j + 1)] = _dot(
                ua_bf[:, LANES * j:LANES * (j + 1)], bmat_ref[j])
        ub = _dot(h, win_ref[:, C_UB:C_ZB])
        full_s[pl.ds(tail_rows + r0, chunk), :] = ub[:, :E_BR] * _sigmoid(ub[:, E_BR:])
        q = _dot(h, win_ref[:, C_Q:C_ZX]) * scale
        for hh in range(N_HEADS):
            q_s[hh, pl.ds(r0, chunk), :] = q[:, HEAD_DIM * hh:HEAD_DIM * (hh + 1)]
        return carry

    lax.fori_loop(0, n_chunks, phase1, 0)

    for j in range(N_BLK):
        sl_s = slice(BLK_STATE * j, BLK_STATE * (j + 1))
        sl_re = slice(2 * BLK_STATE * j, 2 * BLK_STATE * j + BLK_STATE)
        sl_im = slice(2 * BLK_STATE * j + BLK_STATE, 2 * BLK_STATE * (j + 1))
        lam_re = lre_ref[:, sl_s]
        lam_im = lim_ref[:, sl_s]

        def step(t, carry, sl_re=sl_re, sl_im=sl_im, lam_re=lam_re, lam_im=lam_im):
            s_re, s_im = carry
            r = pl.multiple_of(t * n_seq, n_seq)
            n_re = lam_re * s_re - lam_im * s_im + bu_s[pl.ds(r, n_seq), sl_re]
            n_im = lam_re * s_im + lam_im * s_re + bu_s[pl.ds(r, n_seq), sl_im]
            bu_s[pl.ds(r, n_seq), sl_re] = n_re
            bu_s[pl.ds(r, n_seq), sl_im] = n_im
            return n_re, n_im

        s_re, s_im = lax.fori_loop(0, n_t, step, (sre_ref[:, sl_s], sim_ref[:, sl_s]), unroll=4)
        sre_ref[:, sl_s] = s_re
        sim_ref[:, sl_s] = s_im

    pairs = [(b, hh) for b in range(n_seq) for hh in range(N_HEADS)]
    scores = []
    for b, hh in pairs:
        qb = q_s[hh, pl.ds(b, n_t, stride=n_seq), :].astype(_BF16)
        scores.append(lax.dot_general(qb, k_ref[b, :, HEAD_DIM * hh:HEAD_DIM * (hh + 1)],
                                      (((1,), (1,)), ((), ())), preferred_element_type=_F32))
    probs = [jnp.exp(s - jnp.max(s, axis=-1, keepdims=True)).astype(_BF16) for s in scores]
    outs = [_dot(p, va_ref[b, hh]) for p, (b, hh) in zip(probs, pairs)]
    for o, (b, hh) in zip(outs, pairs):
        o_s[hh, pl.ds(b, n_t, stride=n_seq), :] = o[:, :HEAD_DIM] / o[:, HEAD_DIM:]

    def phase3(c, carry):
        r0 = pl.multiple_of(c * chunk, chunk)
        rs = pl.ds(r0, chunk)
        h = h_s[rs, :]
        ya = _ssm_readout(bu_s[rs, :].astype(_BF16), ua_s[rs, :], cmat_ref, d_ref, wglu_ref, bglu_ref)
        ya = ya * _silu(_dot(h, win_ref[:, C_ZA:C_UB]))
        m = _sigmoid(_dot(h, win_ref[:, C_GA:C_GB])) * _dot(ya.astype(_BF16), wbra_ref[...])
        acc = jnp.zeros((chunk, E_BR), _F32) + cb_ref[...]
        for k in range(CONV_K):
            acc = acc + cw_ref[k:k + 1, :] * full_s[pl.ds(r0 + k * n_seq, chunk), :]
        cb = _silu(_layernorm(acc, lng_ref[...], lnb_ref[...]))
        cb = cb * _silu(_dot(h, win_ref[:, C_ZB:C_Q]))
        m = m + _sigmoid(_dot(h, win_ref[:, C_GB:C_GX])) * _dot(cb.astype(_BF16), wbrb_ref[...])
        ox = jnp.concatenate([o_s[hh, rs, :] for hh in range(N_HEADS)], axis=-1)
        ox = ox * _silu(_dot(h, win_ref[:, C_ZX:C_GA]))
        m = m + _sigmoid(_dot(h, win_ref[:, C_GX:C_END])) * _dot(ox.astype(_BF16), wbrx_ref[...])
        out = x_ref[rs, :] + _dot(m.astype(_BF16), wout_ref[...])
        if final:
            out = _rmsnorm(out, fg_ref[...])
        y_ref[rs, :] = out
        return carry

    lax.fori_loop(0, n_chunks, phase3, 0)

    new_tail = full_s[rows:rows + tail_rows, :]
    full_s[0:tail_rows, :] = new_tail

    @pl.when(i == pl.num_programs(0) - 1)
    def _fin():
        tail_ref[...] = new_tail


_PROMPT_WEIGHTS = ('norm_g', 'w_in_bf', 'lam_re8', 'lam_im8', 'bmat_bf', 'cmat_bf', 'd', 'w_glu_bf', 'b_glu',
                   'w_br_ssm_bf', 'conv_w', 'conv_b', 'ln_g', 'ln_b', 'w_br_conv_bf')


def _prompt_layer(x2d, p, layer, kb, va, final_g, *, n_seq, rows, chunk, final):
    total = x2d.shape[0]
    tail_rows = (CONV_K - 1) * n_seq
    kernel = functools.partial(_prompt_kernel, rows=rows, chunk=chunk, n_seq=n_seq, final=final)
    row_spec = pl.BlockSpec((rows, D_MODEL), lambda i: (i, 0))
    stacked = [p[n] for n in _PROMPT_WEIGHTS] + [kb, va, p['w_br_xatt_bf'], p['w_out_bf']]
    state_spec = pl.BlockSpec((n_seq, N_SSM), lambda i: (0, 0))
    return pl.pallas_call(
        kernel,
        grid=(total // rows,),
        in_specs=[row_spec] + [_layer_spec(a, layer) for a in stacked] + [_const_spec(final_g)],
        out_specs=[row_spec, state_spec, state_spec, pl.BlockSpec((tail_rows, E_BR), lambda i: (0, 0))],
        out_shape=[jax.ShapeDtypeStruct((total, D_MODEL), _F32),
                   jax.ShapeDtypeStruct((n_seq, N_SSM), _F32),
                   jax.ShapeDtypeStruct((n_seq, N_SSM), _F32),
                   jax.ShapeDtypeStruct((tail_rows, E_BR), _F32)],
        scratch_shapes=[pltpu.VMEM((rows, D_MODEL), _BF16),
                        pltpu.VMEM((rows, E_BR), _F32),
                        pltpu.VMEM((rows, 2 * N_SSM), _F32),
                        pltpu.VMEM((tail_rows + rows, E_BR), _F32),
                        pltpu.VMEM((N_HEADS, rows, HEAD_DIM), _F32),
                        pltpu.VMEM((N_HEADS, rows, HEAD_DIM), _F32)],
        compiler_params=pltpu.CompilerParams(dimension_semantics=("arbitrary",), vmem_limit_bytes=VMEM_LIMIT),
        name="prompt_layer",
    )(x2d, *stacked, final_g)


def _sample_kernel(x_ref, ng_ref, win_ref, wua_ref, wub_ref, lre_ref, lim_ref, bmat_ref, cmat_ref, d_ref,
                   wglu_ref, bglu_ref, wbra_ref, cw_ref, cb_ref, lng_ref, lnb_ref, wbrb_ref,
                   s0re_ref, s0im_ref, cst_ref, k_ref, v_ref, wbrx_ref, wout_ref, fg_ref,
                   y_ref, sre_ref, sim_ref, vnew_ref,
                   h_s, m_s, q_s, o_s, acc_s,
                   *, n, blk, final):
    i = pl.program_id(0)
    scale = 1.0 / math.sqrt(HEAD_DIM)
    groups = KV_ROWS // SUBLANES

    @pl.when(i == 0)
    def _pre():
        x = x_ref[...]
        hf = _rmsnorm(x, ng_ref[...])
        h = hf.astype(_BF16)
        h_s[...] = h
        ua = _dot_exact(hf, wua_ref[...])
        bu = jnp.concatenate([_dot_exact(ua[:, LANES * j:LANES * (j + 1)], bmat_ref[j]) for j in range(N_BLK)],
                             axis=-1)
        xs_parts = []
        for j in range(N_BLK):
            sl_s = slice(BLK_STATE * j, BLK_STATE * (j + 1))
            b_re = bu[:, 2 * BLK_STATE * j:2 * BLK_STATE * j + BLK_STATE]
            b_im = bu[:, 2 * BLK_STATE * j + BLK_STATE:2 * BLK_STATE * (j + 1)]
            l_re, l_im = lre_ref[:, sl_s], lim_ref[:, sl_s]
            o_re, o_im = s0re_ref[:, sl_s], s0im_ref[:, sl_s]
            n_re = l_re * o_re - l_im * o_im + b_re
            n_im = l_re * o_im + l_im * o_re + b_im
            sre_ref[:, sl_s] = n_re
            sim_ref[:, sl_s] = n_im
            xs_parts += [n_re, n_im]
        xs_bf = jnp.concatenate(xs_parts, axis=-1).astype(_BF16)
        ya = _ssm_readout(xs_bf, ua, cmat_ref, d_ref, wglu_ref, bglu_ref)
        ya = ya * _silu(_dot(h, win_ref[:, C_ZA:C_UB]))
        m = _sigmoid(_dot(h, win_ref[:, C_GA:C_GB])) * _dot(ya.astype(_BF16), wbra_ref[...])
        ub = _dot_exact(hf, wub_ref[...])
        vb = ub[:, :E_BR] * _sigmoid(ub[:, E_BR:])
        vnew_ref[...] = vb

        def conv_hist(b, carry):
            acc_s[pl.ds(b, 1), :] = jnp.sum(cst_ref[b] * cw_ref[0:CONV_K - 1, :], axis=0, keepdims=True)
            return carry

        lax.fori_loop(0, n, conv_hist, 0, unroll=4)
        acc = acc_s[...] + cb_ref[...] + cw_ref[CONV_K - 1:CONV_K, :] * vb
        cb = _silu(_layernorm(acc, lng_ref[...], lnb_ref[...]))
        cb = cb * _silu(_dot(h, win_ref[:, C_ZB:C_Q]))
        m = m + _sigmoid(_dot(h, win_ref[:, C_GB:C_GX])) * _dot(cb.astype(_BF16), wbrb_ref[...])
        m_s[...] = m
        q = _dot(h, win_ref[:, C_Q:C_ZX]) * scale
        for j in range(SUBLANES):
            hh = j % N_HEADS
            q_s[n * j:n * (j + 1), :] = q[:, HEAD_DIM * hh:HEAD_DIM * (hh + 1)]

    ones = jnp.ones((HEAD_DIM, HEAD_DIM), _BF16)

    def fold(t):
        return t, pltpu.roll(t, N_HEADS, axis=0)

    for s in range(blk):
        seq = i * blk + s
        q8 = q_s[pl.ds(seq, SUBLANES, stride=n), :]
        kq = (k_ref[s].reshape(groups, SUBLANES, HEAD_DIM) * q8[None]).astype(_BF16)
        sc = _dot(kq.reshape(KV_ROWS, HEAD_DIM), ones).reshape(groups, SUBLANES, HEAD_DIM)
        a, b2 = fold(jnp.max(sc, axis=0))
        e = jnp.exp(sc - jnp.maximum(a, b2)[None])
        a, b2 = fold(jnp.sum(e, axis=0))
        den = a + b2
        a, b2 = fold(jnp.sum(e * v_ref[s].reshape(groups, SUBLANES, HEAD_DIM), axis=0))
        o8 = (a + b2) / den
        for hh in range(N_HEADS):
            o_s[hh, pl.ds(seq, 1), :] = o8[hh:hh + 1, :]

    @pl.when(i == pl.num_programs(0) - 1)
    def _post():
        h = h_s[...]
        ox = jnp.concatenate([o_s[hh] for hh in range(N_HEADS)], axis=-1)
        ox = ox * _silu(_dot(h, win_ref[:, C_ZX:C_GA]))
        m = m_s[...] + _sigmoid(_dot(h, win_ref[:, C_GX:C_END])) * _dot(ox.astype(_BF16), wbrx_ref[...])
        out = x_ref[...] + _dot(m.astype(_BF16), wout_ref[...])
        if final:
            out = _rmsnorm(out, fg_ref[...])
        y_ref[...] = out


def _sample_layer(x2d, p, w_in_f32, layer, s0re, s0im, cst, kc, vc, final_g, *, blk, final):
    n = x2d.shape[0]
    kernel = functools.partial(_sample_kernel, n=n, blk=blk, final=final)
    kv_spec = pl.BlockSpec((None, blk, KV_ROWS, HEAD_DIM), lambda i, _l=layer: (_l, i, 0, 0))
    ls = lambda a: _layer_spec(a, layer)
    wua_spec = pl.BlockSpec((None, D_MODEL, E_BR), lambda i, _l=layer: (_l, 0, C_UA // E_BR),
                            pipeline_mode=pl.Buffered(1))
    wub_spec = pl.BlockSpec((None, D_MODEL, 2 * E_BR), lambda i, _l=layer: (_l, 0, C_UB // (2 * E_BR)),
                            pipeline_mode=pl.Buffered(1))
    mid = [p[nm] for nm in ('lam_re1', 'lam_im1', 'bmat', 'cmat_bf', 'd', 'w_glu_bf', 'b_glu', 'w_br_ssm_bf',
                            'conv_w', 'conv_b', 'ln_g', 'ln_b', 'w_br_conv_bf')]
    ins = [x2d, p['norm_g'], p['w_in_bf'], w_in_f32, w_in_f32] + mid + [s0re, s0im, cst, kc, vc,
                                                                          p['w_br_xatt_bf'], p['w_out_bf'], final_g]
    in_specs = ([_const_spec(x2d), ls(p['norm_g']), ls(p['w_in_bf']), wua_spec, wub_spec] + [ls(a) for a in mid]
                + [ls(s0re), ls(s0im), ls(cst), kv_spec, kv_spec, ls(p['w_br_xatt_bf']), ls(p['w_out_bf']),
                   _const_spec(final_g)])
    full = lambda shape: pl.BlockSpec(shape, lambda i: (0, 0))
    return pl.pallas_call(
        kernel,
        grid=(n // blk,),
        in_specs=in_specs,
        out_specs=[full((n, D_MODEL)), full((n, N_SSM)), full((n, N_SSM)), full((n, E_BR))],
        out_shape=[jax.ShapeDtypeStruct((n, D_MODEL), _F32),
                   jax.ShapeDtypeStruct((n, N_SSM), _F32),
                   jax.ShapeDtypeStruct((n, N_SSM), _F32),
                   jax.ShapeDtypeStruct((n, E_BR), _F32)],
        scratch_shapes=[pltpu.VMEM((n, D_MODEL), _BF16),
                        pltpu.VMEM((n, D_MODEL), _F32),
                        pltpu.VMEM((SUBLANES * n, HEAD_DIM), _F32),
                        pltpu.VMEM((N_HEADS, n, HEAD_DIM), _F32),
                        pltpu.VMEM((n, E_BR), _F32)],
        compiler_params=pltpu.CompilerParams(dimension_semantics=("arbitrary",), vmem_limit_bytes=VMEM_LIMIT),
        name="sample_layer",
    )(*ins)


def _prep(norm_g, w_in, a_re, a_im, log_dt, b_re, b_im, c_re, c_im, d, w_glu, b_glu, w_br_ssm,
          conv_w, conv_b, ln_g, ln_b, w_br_conv, w_br_xatt, w_out):
    depth = norm_g.shape[0]
    dt = jnp.exp(log_dt)[..., None]
    mag = jnp.exp(a_re * dt)
    lr = mag * jnp.cos(a_im * dt)
    li = mag * jnp.sin(a_im * dt)
    den = a_re * a_re + a_im * a_im
    cr = ((lr - 1.0) * a_re + li * a_im) / den
    ci = (li * a_re - (lr - 1.0) * a_im) / den
    bb_re = cr[..., None] * b_re - ci[..., None] * b_im
    bb_im = cr[..., None] * b_im + ci[..., None] * b_re
    gpb = N_GROUPS // N_BLK
    eye = jnp.eye(gpb, dtype=_F32)

    def b_blocks(bb):
        bb = bb.reshape(depth, N_BLK, gpb, N_STATE, GROUP)
        return jnp.einsum('ljgph,gk->ljghkp', bb, eye).reshape(depth, N_BLK, LANES, BLK_STATE)

    def c_blocks(cc):
        cc = cc.reshape(depth, N_BLK, gpb, GROUP, N_STATE)
        return jnp.einsum('ljghp,gk->ljkpgh', cc, eye).reshape(depth, N_BLK, BLK_STATE, LANES)

    bmat = jnp.concatenate([b_blocks(bb_re), b_blocks(bb_im)], axis=-1)
    cmat = jnp.concatenate([c_blocks(c_re), -c_blocks(c_im)], axis=2)
    lr1 = lr.reshape(depth, 1, N_SSM)
    li1 = li.reshape(depth, 1, N_SSM)
    row = lambda v: v.reshape(depth, 1, -1)
    return {
        'norm_g': row(norm_g), 'w_in_bf': w_in.astype(_BF16),
        'lam_re1': lr1, 'lam_im1': li1,
        'lam_re8': jnp.broadcast_to(lr1, (depth, SUBLANES, N_SSM)),
        'lam_im8': jnp.broadcast_to(li1, (depth, SUBLANES, N_SSM)),
        'bmat': bmat, 'bmat_bf': bmat.astype(_BF16), 'cmat_bf': cmat.astype(_BF16),
        'd': row(d), 'w_glu_bf': w_glu.astype(_BF16), 'b_glu': row(b_glu),
        'w_br_ssm_bf': w_br_ssm.astype(_BF16), 'conv_w': conv_w, 'conv_b': row(conv_b),
        'ln_g': row(ln_g), 'ln_b': row(ln_b), 'w_br_conv_bf': w_br_conv.astype(_BF16),
        'w_br_xatt_bf': w_br_xatt.astype(_BF16), 'w_out_bf': w_out.astype(_BF16),
    }


def kernel(x_prompt, x_sample, mem_prompt, state_ssm_re, state_ssm_im, state_conv, cache_mem_k, cache_mem_v,
           norm_g, w_in, ssm_a_re, ssm_a_im, ssm_log_dt, ssm_b_re, ssm_b_im, ssm_c_re, ssm_c_im, ssm_d,
           w_glu, b_glu, w_br_ssm, conv_w, conv_b, conv_ln_g, conv_ln_b, w_br_conv,
           mem_norm_g, w_k, w_v, w_br_xatt, w_out, final_norm_g):
    bp, seq, _ = x_prompt.shape
    bs = x_sample.shape[0]
    depth = norm_g.shape[0]
    assert bp == SUBLANES and x_sample.shape[1] == 1 and depth == DEPTH
    rows, chunk, sample_blk = 512, 256, 4

    p = _prep(norm_g, w_in, ssm_a_re, ssm_a_im, ssm_log_dt, ssm_b_re, ssm_b_im, ssm_c_re, ssm_c_im, ssm_d,
              w_glu, b_glu, w_br_ssm, conv_w, conv_b, conv_ln_g, conv_ln_b, w_br_conv, w_br_xatt, w_out)
    final_g = final_norm_g.reshape(1, D_MODEL)
    wkv = jnp.concatenate([w_k, w_v], axis=-1).astype(_BF16)
    mem_k, mem_v, kb, va = _memkv(mem_prompt, mem_norm_g.reshape(depth, 1, D_MODEL), wkv)

    xp = x_prompt.transpose(1, 0, 2).reshape(seq * bp, D_MODEL)
    xs = x_sample.reshape(bs, D_MODEL)
    s0re = state_ssm_re.reshape(depth, bs, N_SSM)
    s0im = state_ssm_im.reshape(depth, bs, N_SSM)
    kc = cache_mem_k.reshape(depth, bs, KV_ROWS, HEAD_DIM)
    vc = cache_mem_v.reshape(depth, bs, KV_ROWS, HEAD_DIM)

    p_re, p_im, p_conv, s_re, s_im, s_conv = [], [], [], [], [], []
    for l in range(depth):
        final = l == depth - 1
        xp, re, im, tail = _prompt_layer(xp, p, l, kb, va, final_g, n_seq=bp, rows=rows, chunk=chunk, final=final)
        p_re.append(re.reshape(bp, N_GROUPS, N_STATE))
        p_im.append(im.reshape(bp, N_GROUPS, N_STATE))
        p_conv.append(tail.reshape(CONV_K - 1, bp, E_BR).transpose(1, 0, 2))

        xs, re, im, vnew = _sample_layer(xs, p, w_in, l, s0re, s0im, state_conv, kc, vc, final_g,
                                         blk=sample_blk, final=final)
        s_re.append(re.reshape(bs, N_GROUPS, N_STATE))
        s_im.append(im.reshape(bs, N_GROUPS, N_STATE))
        s_conv.append(jnp.concatenate([state_conv[l][:, 1:, :], vnew[:, None, :]], axis=1))

    y_prompt = xp.reshape(seq, bp, D_MODEL).transpose(1, 0, 2)
    y_sample = xs.reshape(bs, 1, D_MODEL)
    kv_shape = (depth, bp, N_MEM, N_HEADS, HEAD_DIM)
    return (y_prompt, y_sample, jnp.stack(p_re), jnp.stack(p_im), jnp.stack(p_conv), mem_k.reshape(kv_shape),
            mem_v.reshape(kv_shape), jnp.stack(s_re), jnp.stack(s_im), jnp.stack(s_conv))
```

```python
import functools
import math

import jax
import jax.numpy as jnp
from jax import lax
from jax.experimental import pallas as pl
from jax.experimental.pallas import tpu as pltpu

D_MODEL = 1024
E_BR = 512
N_GROUPS = 32
GROUP = 16
N_STATE = 64
N_SSM = N_GROUPS * N_STATE
CONV_K = 31
N_HEADS = 4
HEAD_DIM = 128
N_MEM = 256
EPS = 1e-6
DEPTH = 2

LANES = 128
SUBLANES = 8
N_BLK = E_BR // LANES
BLK_STATE = N_SSM // N_BLK
KV_ROWS = N_MEM * N_HEADS

C_UA, C_ZA, C_UB, C_ZB, C_Q, C_ZX, C_GA, C_GB, C_GX, C_END = (
    0, 512, 1024, 2048, 2560, 3072, 3584, 4608, 5632, 6656)

VMEM_LIMIT = 60000 * 1024

_F32 = jnp.float32
_BF16 = jnp.bfloat16


def _dot(a, b):
    return jnp.dot(a, b, preferred_element_type=_F32)


def _dot_exact(a, b):
    return jnp.dot(a, b, preferred_element_type=_F32, precision=lax.Precision.HIGHEST)


def _sigmoid(x):
    return jax.nn.sigmoid(x)


def _silu(x):
    return x * _sigmoid(x)


def _rmsnorm(x, g):
    return x * lax.rsqrt(jnp.mean(x * x, axis=-1, keepdims=True) + EPS) * g


def _layernorm(x, g, b):
    mu = jnp.mean(x, axis=-1, keepdims=True)
    xc = x - mu
    var = jnp.mean(xc * xc, axis=-1, keepdims=True)
    return xc * lax.rsqrt(var + EPS) * g + b


def _layer_spec(arr, layer):
    nd = arr.ndim - 1
    return pl.BlockSpec((None,) + arr.shape[1:], lambda *_, _l=layer, _nd=nd: (_l,) + (0,) * _nd,
                        pipeline_mode=pl.Buffered(1))


def _const_spec(arr):
    nd = arr.ndim
    return pl.BlockSpec(arr.shape, lambda *_, _nd=nd: (0,) * _nd, pipeline_mode=pl.Buffered(1))


def _memkv_kernel(mem_ref, g_ref, wkv_ref, k_ref, v_ref, kb_ref, va_ref):
    hm = _rmsnorm(mem_ref[...], g_ref[...]).astype(_BF16)
    kv = _dot(hm, wkv_ref[...])
    kb_ref[...] = kv[:, :E_BR].astype(_BF16)
    ones = jnp.ones((N_MEM, HEAD_DIM), _BF16)
    for hh in range(N_HEADS):
        kh = kv[:, HEAD_DIM * hh:HEAD_DIM * (hh + 1)]
        vh = kv[:, E_BR + HEAD_DIM * hh:E_BR + HEAD_DIM * (hh + 1)]
        k_ref[pl.ds(hh, N_MEM, stride=N_HEADS), :] = kh
        v_ref[pl.ds(hh, N_MEM, stride=N_HEADS), :] = vh
        va_ref[hh, :, 0:HEAD_DIM] = vh.astype(_BF16)
        va_ref[hh, :, HEAD_DIM:2 * HEAD_DIM] = ones


def _memkv(mem, g, wkv):
    depth, bp = g.shape[0], mem.shape[0]
    kv_f = jax.ShapeDtypeStruct((depth, bp, KV_ROWS, HEAD_DIM), _F32)
    kv_spec = pl.BlockSpec((None, None, KV_ROWS, HEAD_DIM), lambda l, b: (l, b, 0, 0))
    return pl.pallas_call(
        _memkv_kernel,
        grid=(depth, bp),
        in_specs=[pl.BlockSpec((None, N_MEM, D_MODEL), lambda l, b: (b, 0, 0)),
                  pl.BlockSpec((None, 1, D_MODEL), lambda l, b: (l, 0, 0)),
                  pl.BlockSpec((None, D_MODEL, 2 * E_BR), lambda l, b: (l, 0, 0))],
        out_specs=[kv_spec, kv_spec,
                   pl.BlockSpec((None, None, N_MEM, E_BR), lambda l, b: (l, b, 0, 0)),
                   pl.BlockSpec((None, None, N_HEADS, N_MEM, 2 * HEAD_DIM), lambda l, b: (l, b, 0, 0, 0))],
        out_shape=[kv_f, kv_f,
                   jax.ShapeDtypeStruct((depth, bp, N_MEM, E_BR), _BF16),
                   jax.ShapeDtypeStruct((depth, bp, N_HEADS, N_MEM, 2 * HEAD_DIM), _BF16)],
        compiler_params=pltpu.CompilerParams(dimension_semantics=("arbitrary", "arbitrary")),
        name="mem_kv",
    )(mem, g, wkv)


def _ssm_readout(xs_bf, ua, cmat_ref, d_ref, wglu_ref, bglu_ref):
    ys = [_dot(xs_bf[:, 2 * BLK_STATE * j:2 * BLK_STATE * (j + 1)], cmat_ref[j]) for j in range(N_BLK)]
    y = jnp.concatenate(ys, axis=-1) + d_ref[...] * ua
    y = jax.nn.gelu(y, approximate=True)
    return y * _sigmoid(_dot(y.astype(_BF16), wglu_ref[...]) + bglu_ref[...])


def _conv_chunk(full_s, cw8_ref, r0, chunk, n_seq, piece=64):
    out = []
    for p in range(chunk // piece):
        acc = None
        for k in range(CONV_K):
            w = cw8_ref[SUBLANES * k:SUBLANES * (k + 1), :]
            win = full_s[pl.ds(r0 + p * piece + k * n_seq, piece), :].reshape(piece // SUBLANES, SUBLANES, E_BR)
            term = win * w[None]
            acc = term if acc is None else acc + term
        out.append(acc.reshape(piece, E_BR))
    return jnp.concatenate(out, axis=0)


def _prompt_kernel(x_hbm, ng_ref, win_ref, lre_ref, lim_ref, bmat_ref, cmat_ref, d_ref, wglu_ref, bglu_ref,
                   wbra_ref, cw8_ref, cb_ref, lng_ref, lnb_ref, wbrb_ref, k_ref, va_ref, wbrx_ref, wout_ref,
                   fg_ref,
                   y_hbm, sre_ref, sim_ref, tail_ref,
                   x_buf, y_buf, sem_in, sem_out, h_s, ua_s, bu_s, full_s, q_s, o_s,
                   *, rows, chunk, n_seq, n_tiles, final):
    i = pl.program_id(0)
    n_t = rows // n_seq
    tail_rows = (CONV_K - 1) * n_seq
    n_chunks = rows // chunk
    t_chunk = chunk // n_seq
    scale = 1.0 / math.sqrt(HEAD_DIM)
    slot = lax.rem(i, 2)

    def in_copy(tile, slt, b):
        return pltpu.make_async_copy(x_hbm.at[b, pl.ds(tile * n_t, n_t), :], x_buf.at[slt, :, b, :],
                                     sem_in.at[slt, b])

    def out_copy(tile, slt, b):
        return pltpu.make_async_copy(y_buf.at[slt, :, b, :], y_hbm.at[b, pl.ds(tile * n_t, n_t), :],
                                     sem_out.at[slt, b])

    @pl.when(i == 0)
    def _first_fetch():
        for b in range(n_seq):
            in_copy(0, 0, b).start()

    @pl.when(i + 1 < n_tiles)
    def _prefetch():
        for b in range(n_seq):
            in_copy(i + 1, 1 - slot, b).start()

    for b in range(n_seq):
        in_copy(i, slot, b).wait()

    @pl.when(i >= 2)
    def _drain_old_output():
        for b in range(n_seq):
            out_copy(i - 2, slot, b).wait()

    @pl.when(i == 0)
    def _init():
        sre_ref[...] = jnp.zeros_like(sre_ref)
        sim_ref[...] = jnp.zeros_like(sim_ref)
        full_s[0:tail_rows, :] = jnp.zeros((tail_rows, E_BR), _F32)

    def phase1(c, carry):
        r0 = pl.multiple_of(c * chunk, chunk)
        t0 = pl.multiple_of(c * t_chunk, t_chunk)
        x = x_buf[slot, pl.ds(t0, t_chunk), :, :].reshape(chunk, D_MODEL)
        h = _rmsnorm(x, ng_ref[...]).astype(_BF16)
        h_s[pl.ds(r0, chunk), :] = h
        ua = _dot(h, win_ref[:, C_UA:C_ZA])
        ua_s[pl.ds(r0, chunk), :] = ua
        ua_bf = ua.astype(_BF16)
        for j in range(N_BLK):
            bu_s[pl.ds(r0, chunk), 2 * BLK_STATE * j:2 * BLK_STATE * (j + 1)] = _dot(
                ua_bf[:, LANES * j:LANES * (j + 1)], bmat_ref[j])
        ub = _dot(h, win_ref[:, C_UB:C_ZB])
        full_s[pl.ds(tail_rows + r0, chunk), :] = ub[:, :E_BR] * _sigmoid(ub[:, E_BR:])
        q = _dot(h, win_ref[:, C_Q:C_ZX]) * scale
        for hh in range(N_HEADS):
            q_s[hh, pl.ds(r0, chunk), :] = q[:, HEAD_DIM * hh:HEAD_DIM * (hh + 1)]
        return carry

    lax.fori_loop(0, n_chunks, phase1, 0)

    blk_per_loop = 2
    for j0 in range(0, N_BLK, blk_per_loop):
        js = range(j0, j0 + blk_per_loop)
        lam = [(lre_ref[:, BLK_STATE * j:BLK_STATE * (j + 1)], lim_ref[:, BLK_STATE * j:BLK_STATE * (j + 1)])
               for j in js]

        def step(t, carry, js=js, lam=lam):
            r = pl.ds(pl.multiple_of(t * n_seq, n_seq), n_seq)
            new = []
            for (s_re, s_im), (lam_re, lam_im), j in zip(carry, lam, js):
                sl_re = slice(2 * BLK_STATE * j, 2 * BLK_STATE * j + BLK_STATE)
                sl_im = slice(2 * BLK_STATE * j + BLK_STATE, 2 * BLK_STATE * (j + 1))
                n_re = lam_re * s_re - lam_im * s_im + bu_s[r, sl_re]
                n_im = lam_re * s_im + lam_im * s_re + bu_s[r, sl_im]
                bu_s[r, sl_re] = n_re
                bu_s[r, sl_im] = n_im
                new.append((n_re, n_im))
            return tuple(new)

        init = tuple((sre_ref[:, BLK_STATE * j:BLK_STATE * (j + 1)], sim_ref[:, BLK_STATE * j:BLK_STATE * (j + 1)])
                     for j in js)
        fin = lax.fori_loop(0, n_t, step, init, unroll=2)
        for (s_re, s_im), j in zip(fin, js):
            sre_ref[:, BLK_STATE * j:BLK_STATE * (j + 1)] = s_re
            sim_ref[:, BLK_STATE * j:BLK_STATE * (j + 1)] = s_im

    pairs = [(b, hh) for b in range(n_seq) for hh in range(N_HEADS)]
    scores = []
    for b, hh in pairs:
        qb = q_s[hh, pl.ds(b, n_t, stride=n_seq), :].astype(_BF16)
        scores.append(lax.dot_general(qb, k_ref[b, :, HEAD_DIM * hh:HEAD_DIM * (hh + 1)],
                                      (((1,), (1,)), ((), ())), preferred_element_type=_F32))
    probs = [jnp.exp(s - jnp.max(s, axis=-1, keepdims=True)).astype(_BF16) for s in scores]
    outs = [_dot(p, va_ref[b, hh]) for p, (b, hh) in zip(probs, pairs)]
    for o, (b, hh) in zip(outs, pairs):
        o_s[hh, pl.ds(b, n_t, stride=n_seq), :] = o[:, :HEAD_DIM] / o[:, HEAD_DIM:]

    def phase3(c, carry):
        r0 = pl.multiple_of(c * chunk, chunk)
        rs = pl.ds(r0, chunk)
        h = h_s[rs, :]
        ya = _ssm_readout(bu_s[rs, :].astype(_BF16), ua_s[rs, :], cmat_ref, d_ref, wglu_ref, bglu_ref)
        ya = ya * _silu(_dot(h, win_ref[:, C_ZA:C_UB]))
        m = _sigmoid(_dot(h, win_ref[:, C_GA:C_GB])) * _dot(ya.astype(_BF16), wbra_ref[...])
        acc = _conv_chunk(full_s, cw8_ref, r0, chunk, n_seq) + cb_ref[...]
        cb = _silu(_layernorm(acc, lng_ref[...], lnb_ref[...]))
        cb = cb * _silu(_dot(h, win_ref[:, C_ZB:C_Q]))
        m = m + _sigmoid(_dot(h, win_ref[:, C_GB:C_GX])) * _dot(cb.astype(_BF16), wbrb_ref[...])
        ox = jnp.concatenate([o_s[hh, rs, :] for hh in range(N_HEADS)], axis=-1)
        ox = ox * _silu(_dot(h, win_ref[:, C_ZX:C_GA]))
        m = m + _sigmoid(_dot(h, win_ref[:, C_GX:C_END])) * _dot(ox.astype(_BF16), wbrx_ref[...])
        ts = pl.ds(pl.multiple_of(c * t_chunk, t_chunk), t_chunk)
        out = x_buf[slot, ts, :, :].reshape(chunk, D_MODEL) + _dot(m.astype(_BF16), wout_ref[...])
        if final:
            out = _rmsnorm(out, fg_ref[...])
        y_buf[slot, ts, :, :] = out.reshape(t_chunk, n_seq, D_MODEL)
        return carry

    lax.fori_loop(0, n_chunks, phase3, 0)

    for b in range(n_seq):
        out_copy(i, slot, b).start()

    new_tail = full_s[rows:rows + tail_rows, :]
    full_s[0:tail_rows, :] = new_tail

    @pl.when(i == n_tiles - 1)
    def _fin():
        tail_ref[...] = new_tail
        for b in range(n_seq):
            out_copy(i, slot, b).wait()
        if n_tiles >= 2:
            for b in range(n_seq):
                out_copy(i - 1, 1 - slot, b).wait()


_PROMPT_WEIGHTS = ('norm_g', 'w_in_bf', 'lam_re8', 'lam_im8', 'bmat_bf', 'cmat_bf', 'd', 'w_glu_bf', 'b_glu',
                   'w_br_ssm_bf', 'conv_w8', 'conv_b', 'ln_g', 'ln_b', 'w_br_conv_bf')


def _prompt_layer(x, p, layer, kb, va, final_g, *, rows, chunk, final):
    n_seq, seq, _ = x.shape
    n_t = rows // n_seq
    n_tiles = seq // n_t
    tail_rows = (CONV_K - 1) * n_seq
    kernel = functools.partial(_prompt_kernel, rows=rows, chunk=chunk, n_seq=n_seq, n_tiles=n_tiles, final=final)
    any_spec = pl.BlockSpec(memory_space=pl.ANY)
    stacked = [p[n] for n in _PROMPT_WEIGHTS] + [kb, va, p['w_br_xatt_bf'], p['w_out_bf']]
    state_spec = pl.BlockSpec((n_seq, N_SSM), lambda i: (0, 0))
    return pl.pallas_call(
        kernel,
        grid=(n_tiles,),
        in_specs=[any_spec] + [_layer_spec(a, layer) for a in stacked] + [_const_spec(final_g)],
        out_specs=[any_spec, state_spec, state_spec, pl.BlockSpec((tail_rows, E_BR), lambda i: (0, 0))],
        out_shape=[jax.ShapeDtypeStruct(x.shape, _F32),
                   jax.ShapeDtypeStruct((n_seq, N_SSM), _F32),
                   jax.ShapeDtypeStruct((n_seq, N_SSM), _F32),
                   jax.ShapeDtypeStruct((tail_rows, E_BR), _F32)],
        scratch_shapes=[pltpu.VMEM((2, n_t, n_seq, D_MODEL), _F32),
                        pltpu.VMEM((2, n_t, n_seq, D_MODEL), _F32),
                        pltpu.SemaphoreType.DMA((2, n_seq)),
                        pltpu.SemaphoreType.DMA((2, n_seq)),
                        pltpu.VMEM((rows, D_MODEL), _BF16),
                        pltpu.VMEM((rows, E_BR), _F32),
                        pltpu.VMEM((rows, 2 * N_SSM), _F32),
                        pltpu.VMEM((tail_rows + rows, E_BR), _F32),
                        pltpu.VMEM((N_HEADS, rows, HEAD_DIM), _F32),
                        pltpu.VMEM((N_HEADS, rows, HEAD_DIM), _F32)],
        compiler_params=pltpu.CompilerParams(dimension_semantics=("arbitrary",), vmem_limit_bytes=VMEM_LIMIT),
        name="prompt_layer",
    )(x, *stacked, final_g)


def _sample_kernel(x_ref, ng_ref, win_ref, wua_ref, wub_ref, lre_ref, lim_ref, bmat_ref, cmat_ref, d_ref,
                   wglu_ref, bglu_ref, wbra_ref, cw_ref, cb_ref, lng_ref, lnb_ref, wbrb_ref,
                   s0re_ref, s0im_ref, cst_ref, k_ref, v_ref, wbrx_ref, wout_ref, fg_ref,
                   y_ref, sre_ref, sim_ref, vnew_ref,
                   h_s, m_s, q_s, o_s, acc_s,
                   *, n, blk, final):
    i = pl.program_id(0)
    scale = 1.0 / math.sqrt(HEAD_DIM)
    groups = KV_ROWS // SUBLANES

    @pl.when(i == 0)
    def _pre():
        x = x_ref[...]
        hf = _rmsnorm(x, ng_ref[...])
        h = hf.astype(_BF16)
        h_s[...] = h
        ua = _dot_exact(hf, wua_ref[...])
        bu = jnp.concatenate([_dot_exact(ua[:, LANES * j:LANES * (j + 1)], bmat_ref[j]) for j in range(N_BLK)],
                             axis=-1)
        xs_parts = []
        for j in range(N_BLK):
            sl_s = slice(BLK_STATE * j, BLK_STATE * (j + 1))
            b_re = bu[:, 2 * BLK_STATE * j:2 * BLK_STATE * j + BLK_STATE]
            b_im = bu[:, 2 * BLK_STATE * j + BLK_STATE:2 * BLK_STATE * (j + 1)]
            l_re, l_im = lre_ref[:, sl_s], lim_ref[:, sl_s]
            o_re, o_im = s0re_ref[:, sl_s], s0im_ref[:, sl_s]
            n_re = l_re * o_re - l_im * o_im + b_re
            n_im = l_re * o_im + l_im * o_re + b_im
            sre_ref[:, sl_s] = n_re
            sim_ref[:, sl_s] = n_im
            xs_parts += [n_re, n_im]
        xs_bf = jnp.concatenate(xs_parts, axis=-1).astype(_BF16)
        ya = _ssm_readout(xs_bf, ua, cmat_ref, d_ref, wglu_ref, bglu_ref)
        ya = ya * _silu(_dot(h, win_ref[:, C_ZA:C_UB]))
        m = _sigmoid(_dot(h, win_ref[:, C_GA:C_GB])) * _dot(ya.astype(_BF16), wbra_ref[...])
        ub = _dot_exact(hf, wub_ref[...])
        vb = ub[:, :E_BR] * _sigmoid(ub[:, E_BR:])
        vnew_ref[...] = vb

        def conv_hist(b, carry):
            acc_s[pl.ds(b, 1), :] = jnp.sum(cst_ref[b] * cw_ref[0:CONV_K - 1, :], axis=0, keepdims=True)
            return carry

        lax.fori_loop(0, n, conv_hist, 0, unroll=4)
        acc = acc_s[...] + cb_ref[...] + cw_ref[CONV_K - 1:CONV_K, :] * vb
        cb = _silu(_layernorm(acc, lng_ref[...], lnb_ref[...]))
        cb = cb * _silu(_dot(h, win_ref[:, C_ZB:C_Q]))
        m = m + _sigmoid(_dot(h, win_ref[:, C_GB:C_GX])) * _dot(cb.astype(_BF16), wbrb_ref[...])
        m_s[...] = m
        q = _dot(h, win_ref[:, C_Q:C_ZX]) * scale
        for j in range(SUBLANES):
            hh = j % N_HEADS
            q_s[n * j:n * (j + 1), :] = q[:, HEAD_DIM * hh:HEAD_DIM * (hh + 1)]

    ones = jnp.ones((HEAD_DIM, HEAD_DIM), _BF16)

    def fold(t):
        return t, pltpu.roll(t, N_HEADS, axis=0)

    for s in range(blk):
        seq = i * blk + s
        q8 = q_s[pl.ds(seq, SUBLANES, stride=n), :]
        kq = (k_ref[s].reshape(groups, SUBLANES, HEAD_DIM) * q8[None]).astype(_BF16)
        sc = _dot(kq.reshape(KV_ROWS, HEAD_DIM), ones).reshape(groups, SUBLANES, HEAD_DIM)
        a, b2 = fold(jnp.max(sc, axis=0))
        e = jnp.exp(sc - jnp.maximum(a, b2)[None])
        a, b2 = fold(jnp.sum(e, axis=0))
        den = a + b2
        a, b2 = fold(jnp.sum(e * v_ref[s].reshape(groups, SUBLANES, HEAD_DIM), axis=0))
        o8 = (a + b2) / den
        for hh in range(N_HEADS):
            o_s[hh, pl.ds(seq, 1), :] = o8[hh:hh + 1, :]

    @pl.when(i == pl.num_programs(0) - 1)
    def _post():
        h = h_s[...]
        ox = jnp.concatenate([o_s[hh] for hh in range(N_HEADS)], axis=-1)
        ox = ox * _silu(_dot(h, win_ref[:, C_ZX:C_GA]))
        m = m_s[...] + _sigmoid(_dot(h, win_ref[:, C_GX:C_END])) * _dot(ox.astype(_BF16), wbrx_ref[...])
        out = x_ref[...] + _dot(m.astype(_BF16), wout_ref[...])
        if final:
            out = _rmsnorm(out, fg_ref[...])
        y_ref[...] = out


def _sample_layer(x2d, p, w_in_f32, layer, s0re, s0im, cst, kc, vc, final_g, *, blk, final):
    n = x2d.shape[0]
    kernel = functools.partial(_sample_kernel, n=n, blk=blk, final=final)
    kv_spec = pl.BlockSpec((None, blk, KV_ROWS, HEAD_DIM), lambda i, _l=layer: (_l, i, 0, 0))
    ls = lambda a: _layer_spec(a, layer)
    wua_spec = pl.BlockSpec((None, D_MODEL, E_BR), lambda i, _l=layer: (_l, 0, C_UA // E_BR),
                            pipeline_mode=pl.Buffered(1))
    wub_spec = pl.BlockSpec((None, D_MODEL, 2 * E_BR), lambda i, _l=layer: (_l, 0, C_UB // (2 * E_BR)),
                            pipeline_mode=pl.Buffered(1))
    mid = [p[nm] for nm in ('lam_re1', 'lam_im1', 'bmat', 'cmat_bf', 'd', 'w_glu_bf', 'b_glu', 'w_br_ssm_bf',
                            'conv_w', 'conv_b', 'ln_g', 'ln_b', 'w_br_conv_bf')]
    ins = [x2d, p['norm_g'], p['w_in_bf'], w_in_f32, w_in_f32] + mid + [s0re, s0im, cst, kc, vc,
                                                                          p['w_br_xatt_bf'], p['w_out_bf'], final_g]
    in_specs = ([_const_spec(x2d), ls(p['norm_g']), ls(p['w_in_bf']), wua_spec, wub_spec] + [ls(a) for a in mid]
                + [ls(s0re), ls(s0im), ls(cst), kv_spec, kv_spec, ls(p['w_br_xatt_bf']), ls(p['w_out_bf']),
                   _const_spec(final_g)])
    full = lambda shape: pl.BlockSpec(shape, lambda i: (0, 0))
    return pl.pallas_call(
        kernel,
        grid=(n // blk,),
        in_specs=in_specs,
        out_specs=[full((n, D_MODEL)), full((n, N_SSM)), full((n, N_SSM)), full((n, E_BR))],
        out_shape=[jax.ShapeDtypeStruct((n, D_MODEL), _F32),
                   jax.ShapeDtypeStruct((n, N_SSM), _F32),
                   jax.ShapeDtypeStruct((n, N_SSM), _F32),
                   jax.ShapeDtypeStruct((n, E_BR), _F32)],
        scratch_shapes=[pltpu.VMEM((n, D_MODEL), _BF16),
                        pltpu.VMEM((n, D_MODEL), _F32),
                        pltpu.VMEM((SUBLANES * n, HEAD_DIM), _F32),
                        pltpu.VMEM((N_HEADS, n, HEAD_DIM), _F32),
                        pltpu.VMEM((n, E_BR), _F32)],
        compiler_params=pltpu.CompilerParams(dimension_semantics=("arbitrary",), vmem_limit_bytes=VMEM_LIMIT),
        name="sample_layer",
    )(*ins)


def _prep(norm_g, w_in, a_re, a_im, log_dt, b_re, b_im, c_re, c_im, d, w_glu, b_glu, w_br_ssm,
          conv_w, conv_b, ln_g, ln_b, w_br_conv, w_br_xatt, w_out):
    depth = norm_g.shape[0]
    dt = jnp.exp(log_dt)[..., None]
    mag = jnp.exp(a_re * dt)
    lr = mag * jnp.cos(a_im * dt)
    li = mag * jnp.sin(a_im * dt)
    den = a_re * a_re + a_im * a_im
    cr = ((lr - 1.0) * a_re + li * a_im) / den
    ci = (li * a_re - (lr - 1.0) * a_im) / den
    bb_re = cr[..., None] * b_re - ci[..., None] * b_im
    bb_im = cr[..., None] * b_im + ci[..., None] * b_re
    gpb = N_GROUPS // N_BLK
    eye = jnp.eye(gpb, dtype=_F32)

    def b_blocks(bb):
        bb = bb.reshape(depth, N_BLK, gpb, N_STATE, GROUP)
        return jnp.einsum('ljgph,gk->ljghkp', bb, eye).reshape(depth, N_BLK, LANES, BLK_STATE)

    def c_blocks(cc):
        cc = cc.reshape(depth, N_BLK, gpb, GROUP, N_STATE)
        return jnp.einsum('ljghp,gk->ljkpgh', cc, eye).reshape(depth, N_BLK, BLK_STATE, LANES)

    bmat = jnp.concatenate([b_blocks(bb_re), b_blocks(bb_im)], axis=-1)
    cmat = jnp.concatenate([c_blocks(c_re), -c_blocks(c_im)], axis=2)
    lr1 = lr.reshape(depth, 1, N_SSM)
    li1 = li.reshape(depth, 1, N_SSM)
    row = lambda v: v.reshape(depth, 1, -1)
    return {
        'norm_g': row(norm_g), 'w_in_bf': w_in.astype(_BF16),
        'lam_re1': lr1, 'lam_im1': li1,
        'lam_re8': jnp.broadcast_to(lr1, (depth, SUBLANES, N_SSM)),
        'lam_im8': jnp.broadcast_to(li1, (depth, SUBLANES, N_SSM)),
        'bmat': bmat, 'bmat_bf': bmat.astype(_BF16), 'cmat_bf': cmat.astype(_BF16),
        'd': row(d), 'w_glu_bf': w_glu.astype(_BF16), 'b_glu': row(b_glu),
        'w_br_ssm_bf': w_br_ssm.astype(_BF16), 'conv_w': conv_w, 'conv_w8': jnp.repeat(conv_w, SUBLANES, axis=1),
        'conv_b': row(conv_b),
        'ln_g': row(ln_g), 'ln_b': row(ln_b), 'w_br_conv_bf': w_br_conv.astype(_BF16),
        'w_br_xatt_bf': w_br_xatt.astype(_BF16), 'w_out_bf': w_out.astype(_BF16),
    }


def kernel(x_prompt, x_sample, mem_prompt, state_ssm_re, state_ssm_im, state_conv, cache_mem_k, cache_mem_v,
           norm_g, w_in, ssm_a_re, ssm_a_im, ssm_log_dt, ssm_b_re, ssm_b_im, ssm_c_re, ssm_c_im, ssm_d,
           w_glu, b_glu, w_br_ssm, conv_w, conv_b, conv_ln_g, conv_ln_b, w_br_conv,
           mem_norm_g, w_k, w_v, w_br_xatt, w_out, final_norm_g):
    bp, seq, _ = x_prompt.shape
    bs = x_sample.shape[0]
    depth = norm_g.shape[0]
    assert bp == SUBLANES and x_sample.shape[1] == 1 and depth == DEPTH
    rows, chunk, sample_blk = 512, 256, 4

    p = _prep(norm_g, w_in, ssm_a_re, ssm_a_im, ssm_log_dt, ssm_b_re, ssm_b_im, ssm_c_re, ssm_c_im, ssm_d,
              w_glu, b_glu, w_br_ssm, conv_w, conv_b, conv_ln_g, conv_ln_b, w_br_conv, w_br_xatt, w_out)
    final_g = final_norm_g.reshape(1, D_MODEL)
    wkv = jnp.concatenate([w_k, w_v], axis=-1).astype(_BF16)
    mem_k, mem_v, kb, va = _memkv(mem_prompt, mem_norm_g.reshape(depth, 1, D_MODEL), wkv)

    xp = x_prompt
    xs = x_sample.reshape(bs, D_MODEL)
    s0re = state_ssm_re.reshape(depth, bs, N_SSM)
    s0im = state_ssm_im.reshape(depth, bs, N_SSM)
    kc = cache_mem_k.reshape(depth, bs, KV_ROWS, HEAD_DIM)
    vc = cache_mem_v.reshape(depth, bs, KV_ROWS, HEAD_DIM)

    p_re, p_im, p_conv, s_re, s_im, s_conv = [], [], [], [], [], []
    for l in range(depth):
        final = l == depth - 1
        xp, re, im, tail = _prompt_layer(xp, p, l, kb, va, final_g, rows=rows, chunk=chunk, final=final)
        p_re.append(re.reshape(bp, N_GROUPS, N_STATE))
        p_im.append(im.reshape(bp, N_GROUPS, N_STATE))
        p_conv.append(tail.reshape(CONV_K - 1, bp, E_BR).transpose(1, 0, 2))

        xs, re, im, vnew = _sample_layer(xs, p, w_in, l, s0re, s0im, state_conv, kc, vc, final_g,
                                         blk=sample_blk, final=final)
        s_re.append(re.reshape(bs, N_GROUPS, N_STATE))
        s_im.append(im.reshape(bs, N_GROUPS, N_STATE))
        s_conv.append(jnp.concatenate([state_conv[l][:, 1:, :], vnew[:, None, :]], axis=1))

    y_prompt = xp
    y_sample = xs.reshape(bs, 1, D_MODEL)
    kv_shape = (depth, bp, N_MEM, N_HEADS, HEAD_DIM)
    return (y_prompt, y_sample, jnp.stack(p_re), jnp.stack(p_im), jnp.stack(p_conv), mem_k.reshape(kv_shape),
            mem_v.reshape(kv_shape), jnp.stack(s_re), jnp.stack(s_im), jnp.stack(s_conv))
```

```python
import functools
import math

import jax
import jax.numpy as jnp
from jax import lax
from jax.experimental import pallas as pl
from jax.experimental.pallas import tpu as pltpu

D_MODEL = 1024
E_BR = 512
N_GROUPS = 32
GROUP = 16
N_STATE = 64
N_SSM = N_GROUPS * N_STATE
CONV_K = 31
N_HEADS = 4
HEAD_DIM = 128
N_MEM = 256
EPS = 1e-6
DEPTH = 2

LANES = 128
SUBLANES = 8
N_BLK = E_BR // LANES
BLK_STATE = N_SSM // N_BLK
KV_ROWS = N_MEM * N_HEADS

C_UA, C_ZA, C_UB, C_ZB, C_Q, C_ZX, C_GA, C_GB, C_GX, C_END = (
    0, 512, 1024, 2048, 2560, 3072, 3584, 4608, 5632, 6656)

VMEM_LIMIT = 60000 * 1024

_F32 = jnp.float32
_BF16 = jnp.bfloat16


def _dot(a, b):
    return jnp.dot(a, b, preferred_element_type=_F32)


def _sigmoid(x):
    return jax.nn.sigmoid(x)


def _silu(x):
    return x * _sigmoid(x)


def _rmsnorm(x, g):
    return x * lax.rsqrt(jnp.mean(x * x, axis=-1, keepdims=True) + EPS) * g


def _layernorm(x, g, b):
    mu = jnp.mean(x, axis=-1, keepdims=True)
    xc = x - mu
    var = jnp.mean(xc * xc, axis=-1, keepdims=True)
    return xc * lax.rsqrt(var + EPS) * g + b


def _layer_spec(arr, layer):
    nd = arr.ndim - 1
    return pl.BlockSpec((None,) + arr.shape[1:], lambda *_, _l=layer, _nd=nd: (_l,) + (0,) * _nd,
                        pipeline_mode=pl.Buffered(1))


def _const_spec(arr):
    nd = arr.ndim
    return pl.BlockSpec(arr.shape, lambda *_, _nd=nd: (0,) * _nd, pipeline_mode=pl.Buffered(1))


def _memkv_kernel(mem_ref, g_ref, wkv_ref, k_ref, v_ref, kb_ref, va_ref):
    hm = _rmsnorm(mem_ref[...], g_ref[...]).astype(_BF16)
    kv = _dot(hm, wkv_ref[...])
    kb_ref[...] = kv[:, :E_BR].astype(_BF16)
    ones = jnp.ones((N_MEM, HEAD_DIM), _BF16)
    for hh in range(N_HEADS):
        kh = kv[:, HEAD_DIM * hh:HEAD_DIM * (hh + 1)]
        vh = kv[:, E_BR + HEAD_DIM * hh:E_BR + HEAD_DIM * (hh + 1)]
        k_ref[pl.ds(hh, N_MEM, stride=N_HEADS), :] = kh
        v_ref[pl.ds(hh, N_MEM, stride=N_HEADS), :] = vh
        va_ref[hh, :, 0:HEAD_DIM] = vh.astype(_BF16)
        va_ref[hh, :, HEAD_DIM:2 * HEAD_DIM] = ones


def _memkv(mem, g, wkv):
    depth, bp = g.shape[0], mem.shape[0]
    kv_f = jax.ShapeDtypeStruct((depth, bp, KV_ROWS, HEAD_DIM), _F32)
    kv_spec = pl.BlockSpec((None, None, KV_ROWS, HEAD_DIM), lambda l, b: (l, b, 0, 0))
    return pl.pallas_call(
        _memkv_kernel,
        grid=(depth, bp),
        in_specs=[pl.BlockSpec((None, N_MEM, D_MODEL), lambda l, b: (b, 0, 0)),
                  pl.BlockSpec((None, 1, D_MODEL), lambda l, b: (l, 0, 0)),
                  pl.BlockSpec((None, D_MODEL, 2 * E_BR), lambda l, b: (l, 0, 0))],
        out_specs=[kv_spec, kv_spec,
                   pl.BlockSpec((None, None, N_MEM, E_BR), lambda l, b: (l, b, 0, 0)),
                   pl.BlockSpec((None, None, N_HEADS, N_MEM, 2 * HEAD_DIM), lambda l, b: (l, b, 0, 0, 0))],
        out_shape=[kv_f, kv_f,
                   jax.ShapeDtypeStruct((depth, bp, N_MEM, E_BR), _BF16),
                   jax.ShapeDtypeStruct((depth, bp, N_HEADS, N_MEM, 2 * HEAD_DIM), _BF16)],
        compiler_params=pltpu.CompilerParams(dimension_semantics=("arbitrary", "arbitrary")),
        name="mem_kv",
    )(mem, g, wkv)


def _ssm_readout(xs_bf, ua, cmat_ref, d_ref, wglu_ref, bglu_ref):
    ys = [_dot(xs_bf[:, 2 * BLK_STATE * j:2 * BLK_STATE * (j + 1)], cmat_ref[j]) for j in range(N_BLK)]
    y = jnp.concatenate(ys, axis=-1) + d_ref[...] * ua
    y = jax.nn.gelu(y, approximate=True)
    return y * _sigmoid(_dot(y.astype(_BF16), wglu_ref[...]) + bglu_ref[...])


def _conv_chunk(full_s, cw8_ref, r0, chunk, n_seq, piece=64):
    out = []
    for p in range(chunk // piece):
        acc = None
        for k in range(CONV_K):
            w = cw8_ref[SUBLANES * k:SUBLANES * (k + 1), :]
            win = full_s[pl.ds(r0 + p * piece + k * n_seq, piece), :].reshape(piece // SUBLANES, SUBLANES, E_BR)
            term = win * w[None]
            acc = term if acc is None else acc + term
        out.append(acc.reshape(piece, E_BR))
    return jnp.concatenate(out, axis=0)


def _prompt_kernel(x_hbm, ng_ref, win_ref, lre_ref, lim_ref, bmat_ref, cmat_ref, d_ref, wglu_ref, bglu_ref,
                   wbra_ref, cw8_ref, cb_ref, lng_ref, lnb_ref, wbrb_ref, k_ref, va_ref, wbrx_ref, wout_ref,
                   fg_ref,
                   y_hbm, sre_ref, sim_ref, tail_ref,
                   x_buf, y_buf, sem_in, sem_out, h_s, ua_s, bu_s, full_s, q_s, o_s,
                   *, rows, chunk, n_seq, n_tiles, final):
    i = pl.program_id(0)
    n_t = rows // n_seq
    tail_rows = (CONV_K - 1) * n_seq
    n_chunks = rows // chunk
    t_chunk = chunk // n_seq
    scale = 1.0 / math.sqrt(HEAD_DIM)
    slot = lax.rem(i, 2)

    def in_copy(tile, slt, b):
        return pltpu.make_async_copy(x_hbm.at[b, pl.ds(tile * n_t, n_t), :], x_buf.at[slt, :, b, :],
                                     sem_in.at[slt, b])

    def out_copy(tile, slt, b):
        return pltpu.make_async_copy(y_buf.at[slt, :, b, :], y_hbm.at[b, pl.ds(tile * n_t, n_t), :],
                                     sem_out.at[slt, b])

    @pl.when(i == 0)
    def _first_fetch():
        for b in range(n_seq):
            in_copy(0, 0, b).start()

    @pl.when(i + 1 < n_tiles)
    def _prefetch():
        for b in range(n_seq):
            in_copy(i + 1, 1 - slot, b).start()

    for b in range(n_seq):
        in_copy(i, slot, b).wait()

    @pl.when(i >= 2)
    def _drain_old_output():
        for b in range(n_seq):
            out_copy(i - 2, slot, b).wait()

    @pl.when(i == 0)
    def _init():
        sre_ref[...] = jnp.zeros_like(sre_ref)
        sim_ref[...] = jnp.zeros_like(sim_ref)
        full_s[0:tail_rows, :] = jnp.zeros((tail_rows, E_BR), _F32)

    def phase1(c, carry):
        r0 = pl.multiple_of(c * chunk, chunk)
        t0 = pl.multiple_of(c * t_chunk, t_chunk)
        x = x_buf[slot, pl.ds(t0, t_chunk), :, :].reshape(chunk, D_MODEL)
        h = _rmsnorm(x, ng_ref[...]).astype(_BF16)
        h_s[pl.ds(r0, chunk), :] = h
        ua = _dot(h, win_ref[:, C_UA:C_ZA])
        ua_s[pl.ds(r0, chunk), :] = ua
        ua_bf = ua.astype(_BF16)
        for j in range(N_BLK):
            bu_s[pl.ds(r0, chunk), 2 * BLK_STATE * j:2 * BLK_STATE * (j + 1)] = _dot(
                ua_bf[:, LANES * j:LANES * (j + 1)], bmat_ref[j])
        ub = _dot(h, win_ref[:, C_UB:C_ZB])
        full_s[pl.ds(tail_rows + r0, chunk), :] = ub[:, :E_BR] * _sigmoid(ub[:, E_BR:])
        q = _dot(h, win_ref[:, C_Q:C_ZX]) * scale
        for hh in range(N_HEADS):
            q_s[hh, pl.ds(r0, chunk), :] = q[:, HEAD_DIM * hh:HEAD_DIM * (hh + 1)]
        return carry

    lax.fori_loop(0, n_chunks, phase1, 0)

    blk_per_loop = 2
    for j0 in range(0, N_BLK, blk_per_loop):
        js = range(j0, j0 + blk_per_loop)
        lam = [(lre_ref[:, BLK_STATE * j:BLK_STATE * (j + 1)], lim_ref[:, BLK_STATE * j:BLK_STATE * (j + 1)])
               for j in js]

        def step(t, carry, js=js, lam=lam):
            r = pl.ds(pl.multiple_of(t * n_seq, n_seq), n_seq)
            new = []
            for (s_re, s_im), (lam_re, lam_im), j in zip(carry, lam, js):
                sl_re = slice(2 * BLK_STATE * j, 2 * BLK_STATE * j + BLK_STATE)
                sl_im = slice(2 * BLK_STATE * j + BLK_STATE, 2 * BLK_STATE * (j + 1))
                n_re = lam_re * s_re - lam_im * s_im + bu_s[r, sl_re]
                n_im = lam_re * s_im + lam_im * s_re + bu_s[r, sl_im]
                bu_s[r, sl_re] = n_re
                bu_s[r, sl_im] = n_im
                new.append((n_re, n_im))
            return tuple(new)

        init = tuple((sre_ref[:, BLK_STATE * j:BLK_STATE * (j + 1)], sim_ref[:, BLK_STATE * j:BLK_STATE * (j + 1)])
                     for j in js)
        fin = lax.fori_loop(0, n_t, step, init, unroll=2)
        for (s_re, s_im), j in zip(fin, js):
            sre_ref[:, BLK_STATE * j:BLK_STATE * (j + 1)] = s_re
            sim_ref[:, BLK_STATE * j:BLK_STATE * (j + 1)] = s_im

    pairs = [(b, hh) for b in range(n_seq) for hh in range(N_HEADS)]
    scores = []
    for b, hh in pairs:
        qb = q_s[hh, pl.ds(b, n_t, stride=n_seq), :].astype(_BF16)
        scores.append(lax.dot_general(qb, k_ref[b, :, HEAD_DIM * hh:HEAD_DIM * (hh + 1)],
                                      (((1,), (1,)), ((), ())), preferred_element_type=_F32))
    probs = [jnp.exp(s - jnp.max(s, axis=-1, keepdims=True)).astype(_BF16) for s in scores]
    outs = [_dot(p, va_ref[b, hh]) for p, (b, hh) in zip(probs, pairs)]
    for o, (b, hh) in zip(outs, pairs):
        o_s[hh, pl.ds(b, n_t, stride=n_seq), :] = o[:, :HEAD_DIM] / o[:, HEAD_DIM:]

    def phase3(c, carry):
        r0 = pl.multiple_of(c * chunk, chunk)
        rs = pl.ds(r0, chunk)
        h = h_s[rs, :]
        ya = _ssm_readout(bu_s[rs, :].astype(_BF16), ua_s[rs, :], cmat_ref, d_ref, wglu_ref, bglu_ref)
        ya = ya * _silu(_dot(h, win_ref[:, C_ZA:C_UB]))
        m = _sigmoid(_dot(h, win_ref[:, C_GA:C_GB])) * _dot(ya.astype(_BF16), wbra_ref[...])
        acc = _conv_chunk(full_s, cw8_ref, r0, chunk, n_seq) + cb_ref[...]
        cb = _silu(_layernorm(acc, lng_ref[...], lnb_ref[...]))
        cb = cb * _silu(_dot(h, win_ref[:, C_ZB:C_Q]))
        m = m + _sigmoid(_dot(h, win_ref[:, C_GB:C_GX])) * _dot(cb.astype(_BF16), wbrb_ref[...])
        ox = jnp.concatenate([o_s[hh, rs, :] for hh in range(N_HEADS)], axis=-1)
        ox = ox * _silu(_dot(h, win_ref[:, C_ZX:C_GA]))
        m = m + _sigmoid(_dot(h, win_ref[:, C_GX:C_END])) * _dot(ox.astype(_BF16), wbrx_ref[...])
        ts = pl.ds(pl.multiple_of(c * t_chunk, t_chunk), t_chunk)
        out = x_buf[slot, ts, :, :].reshape(chunk, D_MODEL) + _dot(m.astype(_BF16), wout_ref[...])
        if final:
            out = _rmsnorm(out, fg_ref[...])
        y_buf[slot, ts, :, :] = out.reshape(t_chunk, n_seq, D_MODEL)
        return carry

    lax.fori_loop(0, n_chunks, phase3, 0)

    for b in range(n_seq):
        out_copy(i, slot, b).start()

    new_tail = full_s[rows:rows + tail_rows, :]
    full_s[0:tail_rows, :] = new_tail

    @pl.when(i == n_tiles - 1)
    def _fin():
        tail_ref[...] = new_tail
        for b in range(n_seq):
            out_copy(i, slot, b).wait()
        if n_tiles >= 2:
            for b in range(n_seq):
                out_copy(i - 1, 1 - slot, b).wait()


_PROMPT_WEIGHTS = ('norm_g', 'w_in_bf', 'lam_re8', 'lam_im8', 'bmat_bf', 'cmat_bf', 'd', 'w_glu_bf', 'b_glu',
                   'w_br_ssm_bf', 'conv_w8', 'conv_b', 'ln_g', 'ln_b', 'w_br_conv_bf')


def _prompt_layer(x, p, layer, kb, va, final_g, *, rows, chunk, final):
    n_seq, seq, _ = x.shape
    n_t = rows // n_seq
    n_tiles = seq // n_t
    tail_rows = (CONV_K - 1) * n_seq
    kernel = functools.partial(_prompt_kernel, rows=rows, chunk=chunk, n_seq=n_seq, n_tiles=n_tiles, final=final)
    any_spec = pl.BlockSpec(memory_space=pl.ANY)
    stacked = [p[n] for n in _PROMPT_WEIGHTS] + [kb, va, p['w_br_xatt_bf'], p['w_out_bf']]
    state_spec = pl.BlockSpec((n_seq, N_SSM), lambda i: (0, 0))
    return pl.pallas_call(
        kernel,
        grid=(n_tiles,),
        in_specs=[any_spec] + [_layer_spec(a, layer) for a in stacked] + [_const_spec(final_g)],
        out_specs=[any_spec, state_spec, state_spec, pl.BlockSpec((tail_rows, E_BR), lambda i: (0, 0))],
        out_shape=[jax.ShapeDtypeStruct(x.shape, _F32),
                   jax.ShapeDtypeStruct((n_seq, N_SSM), _F32),
                   jax.ShapeDtypeStruct((n_seq, N_SSM), _F32),
                   jax.ShapeDtypeStruct((tail_rows, E_BR), _F32)],
        scratch_shapes=[pltpu.VMEM((2, n_t, n_seq, D_MODEL), _F32),
                        pltpu.VMEM((2, n_t, n_seq, D_MODEL), _F32),
                        pltpu.SemaphoreType.DMA((2, n_seq)),
                        pltpu.SemaphoreType.DMA((2, n_seq)),
                        pltpu.VMEM((rows, D_MODEL), _BF16),
                        pltpu.VMEM((rows, E_BR), _F32),
                        pltpu.VMEM((rows, 2 * N_SSM), _F32),
                        pltpu.VMEM((tail_rows + rows, E_BR), _F32),
                        pltpu.VMEM((N_HEADS, rows, HEAD_DIM), _F32),
                        pltpu.VMEM((N_HEADS, rows, HEAD_DIM), _F32)],
        compiler_params=pltpu.CompilerParams(dimension_semantics=("arbitrary",), vmem_limit_bytes=VMEM_LIMIT),
        name="prompt_layer",
    )(x, *stacked, final_g)


def _sample_kernel(x_ref, ng_ref, win_ref, lre_ref, lim_ref, bmat_ref, cmat_ref, d_ref,
                   wglu_ref, bglu_ref, wbra_ref, cw_ref, cb_ref, lng_ref, lnb_ref, wbrb_ref,
                   s0re_ref, s0im_ref, cst_ref, k_ref, v_ref, wbrx_ref, wout_ref, fg_ref,
                   y_ref, sre_ref, sim_ref, cnew_ref,
                   x_s, h_s, m_s, q_s, o_s, vb_s, acc_s,
                   *, n, blk):
    layer = pl.program_id(0)
    i = pl.program_id(1)
    scale = 1.0 / math.sqrt(HEAD_DIM)
    groups = KV_ROWS // SUBLANES

    @pl.when((layer == 0) & (i == 0))
    def _load_x():
        x_s[...] = x_ref[...]

    @pl.when(i == 0)
    def _pre():
        h = _rmsnorm(x_s[...], ng_ref[...]).astype(_BF16)
        h_s[...] = h
        ua = _dot(h, win_ref[:, C_UA:C_ZA])
        ua_bf = ua.astype(_BF16)
        xs_parts = []
        for j in range(N_BLK):
            sl_s = slice(BLK_STATE * j, BLK_STATE * (j + 1))
            bu = _dot(ua_bf[:, LANES * j:LANES * (j + 1)], bmat_ref[j])
            l_re, l_im = lre_ref[:, sl_s], lim_ref[:, sl_s]
            o_re, o_im = s0re_ref[:, sl_s], s0im_ref[:, sl_s]
            n_re = l_re * o_re - l_im * o_im + bu[:, :BLK_STATE]
            n_im = l_re * o_im + l_im * o_re + bu[:, BLK_STATE:]
            sre_ref[:, sl_s] = n_re
            sim_ref[:, sl_s] = n_im
            xs_parts += [n_re, n_im]
        xs_bf = jnp.concatenate(xs_parts, axis=-1).astype(_BF16)
        ya = _ssm_readout(xs_bf, ua, cmat_ref, d_ref, wglu_ref, bglu_ref)
        ya = ya * _silu(_dot(h, win_ref[:, C_ZA:C_UB]))
        m_s[...] = _sigmoid(_dot(h, win_ref[:, C_GA:C_GB])) * _dot(ya.astype(_BF16), wbra_ref[...])
        ub = _dot(h, win_ref[:, C_UB:C_ZB])
        vb_s[...] = ub[:, :E_BR] * _sigmoid(ub[:, E_BR:])
        q = _dot(h, win_ref[:, C_Q:C_ZX]) * scale
        for j in range(SUBLANES):
            hh = j % N_HEADS
            q_s[n * j:n * (j + 1), :] = q[:, HEAD_DIM * hh:HEAD_DIM * (hh + 1)]

    ones = jnp.ones((HEAD_DIM, HEAD_DIM), _BF16)

    def fold(t):
        return t, pltpu.roll(t, N_HEADS, axis=0)

    for s in range(blk):
        seq = i * blk + s
        row = pl.ds(seq, 1)
        q8 = q_s[pl.ds(seq, SUBLANES, stride=n), :]
        kq = (k_ref[s].reshape(groups, SUBLANES, HEAD_DIM) * q8[None]).astype(_BF16)
        sc = _dot(kq.reshape(KV_ROWS, HEAD_DIM), ones).reshape(groups, SUBLANES, HEAD_DIM)
        a, b2 = fold(jnp.max(sc, axis=0))
        e = jnp.exp(sc - jnp.maximum(a, b2)[None])
        a, b2 = fold(jnp.sum(e, axis=0))
        den = a + b2
        a, b2 = fold(jnp.sum(e * v_ref[s].reshape(groups, SUBLANES, HEAD_DIM), axis=0))
        o8 = (a + b2) / den
        for hh in range(N_HEADS):
            o_s[hh, row, :] = o8[hh:hh + 1, :]
        acc_s[row, :] = jnp.sum(cst_ref[s] * cw_ref[0:CONV_K - 1, :], axis=0, keepdims=True)
        cnew_ref[s, 0:CONV_K - 2, :] = cst_ref[s, 1:CONV_K - 1, :]
        cnew_ref[s, CONV_K - 2:CONV_K - 1, :] = vb_s[row, :]

    @pl.when(i == pl.num_programs(1) - 1)
    def _post():
        h = h_s[...]
        acc = acc_s[...] + cb_ref[...] + cw_ref[CONV_K - 1:CONV_K, :] * vb_s[...]
        cb = _silu(_layernorm(acc, lng_ref[...], lnb_ref[...]))
        cb = cb * _silu(_dot(h, win_ref[:, C_ZB:C_Q]))
        m = m_s[...] + _sigmoid(_dot(h, win_ref[:, C_GB:C_GX])) * _dot(cb.astype(_BF16), wbrb_ref[...])
        ox = jnp.concatenate([o_s[hh] for hh in range(N_HEADS)], axis=-1)
        ox = ox * _silu(_dot(h, win_ref[:, C_ZX:C_GA]))
        m = m + _sigmoid(_dot(h, win_ref[:, C_GX:C_END])) * _dot(ox.astype(_BF16), wbrx_ref[...])
        out = x_s[...] + _dot(m.astype(_BF16), wout_ref[...])
        x_s[...] = out

        @pl.when(layer == pl.num_programs(0) - 1)
        def _final():
            y_ref[...] = _rmsnorm(out, fg_ref[...])


def _sample_trunk(x2d, p, s0re, s0im, cst, kc, vc, final_g, *, blk):
    n = x2d.shape[0]
    depth = cst.shape[0]
    kernel = functools.partial(_sample_kernel, n=n, blk=blk)

    def per_layer(arr):
        nd = arr.ndim - 1
        return pl.BlockSpec((None,) + arr.shape[1:], lambda l, i, _nd=nd: (l,) + (0,) * _nd,
                            pipeline_mode=pl.Buffered(1))

    blk_spec = lambda rest: pl.BlockSpec((None, blk) + rest, lambda l, i: (l, i, 0, 0))
    kv_spec = blk_spec((KV_ROWS, HEAD_DIM))
    win_spec = blk_spec((CONV_K - 1, E_BR))
    mid = [p[nm] for nm in ('lam_re1', 'lam_im1', 'bmat_bf', 'cmat_bf', 'd', 'w_glu_bf', 'b_glu', 'w_br_ssm_bf',
                            'conv_w', 'conv_b', 'ln_g', 'ln_b', 'w_br_conv_bf')]
    ins = [x2d, p['norm_g'], p['w_in_bf']] + mid + [s0re, s0im, cst, kc, vc, p['w_br_xatt_bf'], p['w_out_bf'],
                                                    final_g]
    in_specs = ([_const_spec(x2d), per_layer(p['norm_g']), per_layer(p['w_in_bf'])] + [per_layer(a) for a in mid]
                + [per_layer(s0re), per_layer(s0im), win_spec, kv_spec, kv_spec, per_layer(p['w_br_xatt_bf']),
                   per_layer(p['w_out_bf']), _const_spec(final_g)])
    state_spec = pl.BlockSpec((None, n, N_SSM), lambda l, i: (l, 0, 0))
    return pl.pallas_call(
        kernel,
        grid=(depth, n // blk),
        in_specs=in_specs,
        out_specs=[pl.BlockSpec((n, D_MODEL), lambda l, i: (0, 0)), state_spec, state_spec, win_spec],
        out_shape=[jax.ShapeDtypeStruct((n, D_MODEL), _F32),
                   jax.ShapeDtypeStruct((depth, n, N_SSM), _F32),
                   jax.ShapeDtypeStruct((depth, n, N_SSM), _F32),
                   jax.ShapeDtypeStruct(cst.shape, _F32)],
        scratch_shapes=[pltpu.VMEM((n, D_MODEL), _F32),
                        pltpu.VMEM((n, D_MODEL), _BF16),
                        pltpu.VMEM((n, D_MODEL), _F32),
                        pltpu.VMEM((SUBLANES * n, HEAD_DIM), _F32),
                        pltpu.VMEM((N_HEADS, n, HEAD_DIM), _F32),
                        pltpu.VMEM((n, E_BR), _F32),
                        pltpu.VMEM((n, E_BR), _F32)],
        compiler_params=pltpu.CompilerParams(dimension_semantics=("arbitrary", "arbitrary"),
                                             vmem_limit_bytes=VMEM_LIMIT),
        name="sample_trunk",
    )(*ins)


def _prep(norm_g, w_in, a_re, a_im, log_dt, b_re, b_im, c_re, c_im, d, w_glu, b_glu, w_br_ssm,
          conv_w, conv_b, ln_g, ln_b, w_br_conv, w_br_xatt, w_out):
    depth = norm_g.shape[0]
    dt = jnp.exp(log_dt)[..., None]
    mag = jnp.exp(a_re * dt)
    lr = mag * jnp.cos(a_im * dt)
    li = mag * jnp.sin(a_im * dt)
    den = a_re * a_re + a_im * a_im
    cr = ((lr - 1.0) * a_re + li * a_im) / den
    ci = (li * a_re - (lr - 1.0) * a_im) / den
    bb_re = cr[..., None] * b_re - ci[..., None] * b_im
    bb_im = cr[..., None] * b_im + ci[..., None] * b_re
    gpb = N_GROUPS // N_BLK
    eye = jnp.eye(gpb, dtype=_F32)

    def b_blocks(bb):
        bb = bb.reshape(depth, N_BLK, gpb, N_STATE, GROUP)
        return jnp.einsum('ljgph,gk->ljghkp', bb, eye).reshape(depth, N_BLK, LANES, BLK_STATE)

    def c_blocks(cc):
        cc = cc.reshape(depth, N_BLK, gpb, GROUP, N_STATE)
        return jnp.einsum('ljghp,gk->ljkpgh', cc, eye).reshape(depth, N_BLK, BLK_STATE, LANES)

    bmat = jnp.concatenate([b_blocks(bb_re), b_blocks(bb_im)], axis=-1)
    cmat = jnp.concatenate([c_blocks(c_re), -c_blocks(c_im)], axis=2)
    lr1 = lr.reshape(depth, 1, N_SSM)
    li1 = li.reshape(depth, 1, N_SSM)
    row = lambda v: v.reshape(depth, 1, -1)
    return {
        'norm_g': row(norm_g), 'w_in_bf': w_in.astype(_BF16),
        'lam_re1': lr1, 'lam_im1': li1,
        'lam_re8': jnp.broadcast_to(lr1, (depth, SUBLANES, N_SSM)),
        'lam_im8': jnp.broadcast_to(li1, (depth, SUBLANES, N_SSM)),
        'bmat_bf': bmat.astype(_BF16), 'cmat_bf': cmat.astype(_BF16),
        'd': row(d), 'w_glu_bf': w_glu.astype(_BF16), 'b_glu': row(b_glu),
        'w_br_ssm_bf': w_br_ssm.astype(_BF16), 'conv_w': conv_w, 'conv_w8': jnp.repeat(conv_w, SUBLANES, axis=1),
        'conv_b': row(conv_b),
        'ln_g': row(ln_g), 'ln_b': row(ln_b), 'w_br_conv_bf': w_br_conv.astype(_BF16),
        'w_br_xatt_bf': w_br_xatt.astype(_BF16), 'w_out_bf': w_out.astype(_BF16),
    }


def kernel(x_prompt, x_sample, mem_prompt, state_ssm_re, state_ssm_im, state_conv, cache_mem_k, cache_mem_v,
           norm_g, w_in, ssm_a_re, ssm_a_im, ssm_log_dt, ssm_b_re, ssm_b_im, ssm_c_re, ssm_c_im, ssm_d,
           w_glu, b_glu, w_br_ssm, conv_w, conv_b, conv_ln_g, conv_ln_b, w_br_conv,
           mem_norm_g, w_k, w_v, w_br_xatt, w_out, final_norm_g):
    bp, seq, _ = x_prompt.shape
    bs = x_sample.shape[0]
    depth = norm_g.shape[0]
    assert bp == SUBLANES and x_sample.shape[1] == 1 and depth == DEPTH
    rows, chunk, sample_blk = 512, 256, 8

    p = _prep(norm_g, w_in, ssm_a_re, ssm_a_im, ssm_log_dt, ssm_b_re, ssm_b_im, ssm_c_re, ssm_c_im, ssm_d,
              w_glu, b_glu, w_br_ssm, conv_w, conv_b, conv_ln_g, conv_ln_b, w_br_conv, w_br_xatt, w_out)
    final_g = final_norm_g.reshape(1, D_MODEL)
    wkv = jnp.concatenate([w_k, w_v], axis=-1).astype(_BF16)
    mem_k, mem_v, kb, va = _memkv(mem_prompt, mem_norm_g.reshape(depth, 1, D_MODEL), wkv)

    xp = x_prompt
    xs = x_sample.reshape(bs, D_MODEL)
    s0re = state_ssm_re.reshape(depth, bs, N_SSM)
    s0im = state_ssm_im.reshape(depth, bs, N_SSM)
    kc = cache_mem_k.reshape(depth, bs, KV_ROWS, HEAD_DIM)
    vc = cache_mem_v.reshape(depth, bs, KV_ROWS, HEAD_DIM)

    p_re, p_im, p_conv = [], [], []
    for l in range(depth):
        final = l == depth - 1
        xp, re, im, tail = _prompt_layer(xp, p, l, kb, va, final_g, rows=rows, chunk=chunk, final=final)
        p_re.append(re.reshape(bp, N_GROUPS, N_STATE))
        p_im.append(im.reshape(bp, N_GROUPS, N_STATE))
        p_conv.append(tail.reshape(CONV_K - 1, bp, E_BR).transpose(1, 0, 2))

    ys, s_re, s_im, s_conv = _sample_trunk(xs, p, s0re, s0im, state_conv, kc, vc, final_g, blk=sample_blk)

    y_prompt = xp
    y_sample = ys.reshape(bs, 1, D_MODEL)
    ssm_shape = (depth, bs, N_GROUPS, N_STATE)
    kv_shape = (depth, bp, N_MEM, N_HEADS, HEAD_DIM)
    return (y_prompt, y_sample, jnp.stack(p_re), jnp.stack(p_im), jnp.stack(p_conv), mem_k.reshape(kv_shape),
            mem_v.reshape(kv_shape), s_re.reshape(ssm_shape), s_im.reshape(ssm_shape), s_conv)
```

```python
import functools
import math

import jax
import jax.numpy as jnp
from jax import lax
from jax.experimental import pallas as pl
from jax.experimental.pallas import tpu as pltpu

D_MODEL = 1024
E_BR = 512
N_GROUPS = 32
GROUP = 16
N_STATE = 64
N_SSM = N_GROUPS * N_STATE
CONV_K = 31
N_HEADS = 4
HEAD_DIM = 128
N_MEM = 256
EPS = 1e-6
DEPTH = 2

LANES = 128
SUBLANES = 8
N_BLK = E_BR // LANES
BLK_STATE = N_SSM // N_BLK
KV_ROWS = N_MEM * N_HEADS

C_UA, C_ZA, C_UB, C_ZB, C_Q, C_ZX, C_GA, C_GB, C_GX, C_END = (
    0, 512, 1024, 2048, 2560, 3072, 3584, 4608, 5632, 6656)

VMEM_LIMIT = 60000 * 1024

_F32 = jnp.float32
_BF16 = jnp.bfloat16


def _dot(a, b):
    return jnp.dot(a, b, preferred_element_type=_F32)


def _sigmoid(x):
    return jax.nn.sigmoid(x)


def _silu(x):
    return x * _sigmoid(x)


def _rmsnorm(x, g):
    return x * lax.rsqrt(jnp.mean(x * x, axis=-1, keepdims=True) + EPS) * g


def _layernorm(x, g, b):
    mu = jnp.mean(x, axis=-1, keepdims=True)
    xc = x - mu
    var = jnp.mean(xc * xc, axis=-1, keepdims=True)
    return xc * lax.rsqrt(var + EPS) * g + b


def _layer_spec(arr, layer):
    nd = arr.ndim - 1
    return pl.BlockSpec((None,) + arr.shape[1:], lambda *_, _l=layer, _nd=nd: (_l,) + (0,) * _nd,
                        pipeline_mode=pl.Buffered(1))


def _const_spec(arr):
    nd = arr.ndim
    return pl.BlockSpec(arr.shape, lambda *_, _nd=nd: (0,) * _nd, pipeline_mode=pl.Buffered(1))


def _memkv_kernel(mem_ref, g_ref, wkv_ref, k_ref, v_ref, kb_ref, va_ref):
    hm = _rmsnorm(mem_ref[...], g_ref[...]).astype(_BF16)
    kv = _dot(hm, wkv_ref[...])
    kb_ref[...] = kv[:, :E_BR].astype(_BF16)
    ones = jnp.ones((N_MEM, HEAD_DIM), _BF16)
    for hh in range(N_HEADS):
        kh = kv[:, HEAD_DIM * hh:HEAD_DIM * (hh + 1)]
        vh = kv[:, E_BR + HEAD_DIM * hh:E_BR + HEAD_DIM * (hh + 1)]
        k_ref[pl.ds(hh, N_MEM, stride=N_HEADS), :] = kh
        v_ref[pl.ds(hh, N_MEM, stride=N_HEADS), :] = vh
        va_ref[hh, :, 0:HEAD_DIM] = vh.astype(_BF16)
        va_ref[hh, :, HEAD_DIM:2 * HEAD_DIM] = ones


def _memkv(mem, g, wkv):
    depth, bp = g.shape[0], mem.shape[0]
    kv_f = jax.ShapeDtypeStruct((depth, bp, KV_ROWS, HEAD_DIM), _F32)
    kv_spec = pl.BlockSpec((None, None, KV_ROWS, HEAD_DIM), lambda l, b: (l, b, 0, 0))
    return pl.pallas_call(
        _memkv_kernel,
        grid=(depth, bp),
        in_specs=[pl.BlockSpec((None, N_MEM, D_MODEL), lambda l, b: (b, 0, 0)),
                  pl.BlockSpec((None, 1, D_MODEL), lambda l, b: (l, 0, 0)),
                  pl.BlockSpec((None, D_MODEL, 2 * E_BR), lambda l, b: (l, 0, 0))],
        out_specs=[kv_spec, kv_spec,
                   pl.BlockSpec((None, None, N_MEM, E_BR), lambda l, b: (l, b, 0, 0)),
                   pl.BlockSpec((None, None, N_HEADS, N_MEM, 2 * HEAD_DIM), lambda l, b: (l, b, 0, 0, 0))],
        out_shape=[kv_f, kv_f,
                   jax.ShapeDtypeStruct((depth, bp, N_MEM, E_BR), _BF16),
                   jax.ShapeDtypeStruct((depth, bp, N_HEADS, N_MEM, 2 * HEAD_DIM), _BF16)],
        compiler_params=pltpu.CompilerParams(dimension_semantics=("arbitrary", "arbitrary")),
        name="mem_kv",
    )(mem, g, wkv)


def _ssm_readout(xs_bf, ua, cmat_ref, d_ref, wglu_ref, bglu_ref):
    ys = [_dot(xs_bf[:, 2 * BLK_STATE * j:2 * BLK_STATE * (j + 1)], cmat_ref[j]) for j in range(N_BLK)]
    y = jnp.concatenate(ys, axis=-1) + d_ref[...] * ua
    y = jax.nn.gelu(y, approximate=True)
    return y * _sigmoid(_dot(y.astype(_BF16), wglu_ref[...]) + bglu_ref[...])


def _zero_of(*parts):
    acc = None
    for v in parts:
        for r in range(0, v.shape[0], SUBLANES):
            for c in range(0, v.shape[1], LANES):
                t = v[r:r + SUBLANES, c:c + LANES]
                acc = t if acc is None else acc + t
    return acc * 0.0


def _conv_piece(full_s, cw8_ref, r0, n_seq, piece, prev=None):
    acc = None
    if prev is not None:
        zero = _zero_of(prev)
        acc = jnp.broadcast_to(jnp.concatenate([zero] * (E_BR // LANES), axis=1)[None],
                               (piece // SUBLANES, SUBLANES, E_BR))
    for k in range(CONV_K):
        w = cw8_ref[SUBLANES * k:SUBLANES * (k + 1), :]
        win = full_s[pl.ds(r0 + k * n_seq, piece), :].reshape(piece // SUBLANES, SUBLANES, E_BR)
        term = win * w[None]
        acc = term if acc is None else acc + term
    return acc.reshape(piece, E_BR)


def _after(lhs, *parts):
    acc = _zero_of(*parts)
    zero = jnp.concatenate([acc, acc], axis=0).astype(_BF16)
    top = jnp.concatenate([lhs[0:2 * SUBLANES, 0:LANES] + zero, lhs[0:2 * SUBLANES, LANES:]], axis=1)
    return jnp.concatenate([top, lhs[2 * SUBLANES:]], axis=0)


def _prompt_kernel(x_hbm, ng_ref, win_ref, lre_ref, lim_ref, bmat_ref, cmat_ref, d_ref, wglu_ref, bglu_ref,
                   wbra_ref, cw8_ref, cb_ref, lng_ref, lnb_ref, wbrb_ref, k_ref, va_ref, wbrx_ref, wout_ref,
                   fg_ref,
                   y_hbm, sre_ref, sim_ref, tail_ref,
                   x_buf, y_buf, sem_in, sem_out, h_s, ua_s, za_s, bu_s, full_s, q_s, o_s,
                   *, rows, chunk, n_seq, n_tiles, final):
    i = pl.program_id(0)
    n_t = rows // n_seq
    tail_rows = (CONV_K - 1) * n_seq
    n_chunks = rows // chunk
    t_chunk = chunk // n_seq
    scale = 1.0 / math.sqrt(HEAD_DIM)
    slot = lax.rem(i, 2)

    def in_copy(tile, slt, b):
        return pltpu.make_async_copy(x_hbm.at[b, pl.ds(tile * n_t, n_t), :], x_buf.at[slt, :, b, :],
                                     sem_in.at[slt, b])

    def out_copy(tile, slt, b):
        return pltpu.make_async_copy(y_buf.at[slt, :, b, :], y_hbm.at[b, pl.ds(tile * n_t, n_t), :],
                                     sem_out.at[slt, b])

    @pl.when(i == 0)
    def _first_fetch():
        for b in range(n_seq):
            in_copy(0, 0, b).start()

    @pl.when(i + 1 < n_tiles)
    def _prefetch():
        for b in range(n_seq):
            in_copy(i + 1, 1 - slot, b).start()

    for b in range(n_seq):
        in_copy(i, slot, b).wait()

    @pl.when(i >= 2)
    def _drain_old_output():
        for b in range(n_seq):
            out_copy(i - 2, slot, b).wait()

    @pl.when(i == 0)
    def _init():
        sre_ref[...] = jnp.zeros_like(sre_ref)
        sim_ref[...] = jnp.zeros_like(sim_ref)
        full_s[0:tail_rows, :] = jnp.zeros((tail_rows, E_BR), _F32)

    def scan_block(j, bu, rs):
        sl_s = slice(BLK_STATE * j, BLK_STATE * (j + 1))
        lam_re, lam_im = lre_ref[:, sl_s], lim_ref[:, sl_s]
        s_re, s_im = sre_ref[:, sl_s], sim_ref[:, sl_s]
        res, ims = [], []
        for t in range(t_chunk):
            b_t = bu[n_seq * t:n_seq * (t + 1), :]
            s_re, s_im = (lam_re * s_re - lam_im * s_im + b_t[:, :BLK_STATE],
                          lam_re * s_im + lam_im * s_re + b_t[:, BLK_STATE:])
            res.append(s_re)
            ims.append(s_im)
        sre_ref[:, sl_s] = s_re
        sim_ref[:, sl_s] = s_im
        bu_s[rs, 2 * BLK_STATE * j:2 * BLK_STATE * j + BLK_STATE] = jnp.concatenate(res, axis=0)
        bu_s[rs, 2 * BLK_STATE * j + BLK_STATE:2 * BLK_STATE * (j + 1)] = jnp.concatenate(ims, axis=0)
        return s_re, s_im

    def phase1(c, carry):
        r0 = pl.multiple_of(c * chunk, chunk)
        rs = pl.ds(r0, chunk)
        t0 = pl.multiple_of(c * t_chunk, t_chunk)
        x = x_buf[slot, pl.ds(t0, t_chunk), :, :].reshape(chunk, D_MODEL)
        h = _rmsnorm(x, ng_ref[...]).astype(_BF16)
        h_s[rs, :] = h
        ua = _dot(h, win_ref[:, C_UA:C_ZA])
        ua_s[rs, :] = ua
        ua_bf = ua.astype(_BF16)
        drive = lambda j: _dot(ua_bf[:, LANES * j:LANES * (j + 1)], bmat_ref[j])
        b0, b1 = drive(0), drive(1)
        st = scan_block(0, b0, rs)
        q = _dot(_after(h, *st), win_ref[:, C_Q:C_ZX]) * scale
        for hh in range(N_HEADS):
            q_s[hh, rs, :] = q[:, HEAD_DIM * hh:HEAD_DIM * (hh + 1)]
        b2 = drive(2)
        st = scan_block(1, b1, rs)
        ub_lin = _dot(_after(h, *st), win_ref[:, C_UB:C_UB + E_BR])
        b3 = drive(3)
        st = scan_block(2, b2, rs)
        ub_gate = _dot(_after(h, *st), win_ref[:, C_UB + E_BR:C_ZB])
        full_s[pl.ds(tail_rows + r0, chunk), :] = ub_lin * _sigmoid(ub_gate)
        st = scan_block(3, b3, rs)
        za_s[rs, :] = _silu(_dot(_after(h, *st), win_ref[:, C_ZA:C_UB]))
        return carry

    lax.fori_loop(0, n_chunks, phase1, 0)

    pairs = [(b, hh) for b in range(n_seq) for hh in range(N_HEADS)]
    scores = []
    for b, hh in pairs:
        qb = q_s[hh, pl.ds(b, n_t, stride=n_seq), :].astype(_BF16)
        scores.append(lax.dot_general(qb, k_ref[b, :, HEAD_DIM * hh:HEAD_DIM * (hh + 1)],
                                      (((1,), (1,)), ((), ())), preferred_element_type=_F32))
    probs = [jnp.exp(s - jnp.max(s, axis=-1, keepdims=True)).astype(_BF16) for s in scores]
    outs = [_dot(p, va_ref[b, hh]) for p, (b, hh) in zip(probs, pairs)]
    for o, (b, hh) in zip(outs, pairs):
        o_s[hh, pl.ds(b, n_t, stride=n_seq), :] = o[:, :HEAD_DIM] / o[:, HEAD_DIM:]

    conv_piece = 64

    def phase3(c, carry):
        r0 = pl.multiple_of(c * chunk, chunk)
        rs = pl.ds(r0, chunk)
        h = h_s[rs, :]
        cvs = []

        def conv_upto(n):
            while len(cvs) < n:
                cvs.append(_conv_piece(full_s, cw8_ref, r0 + len(cvs) * conv_piece, n_seq, conv_piece,
                                       cvs[-1] if cvs else None))
            return cvs[-1]

        per_gate = chunk // conv_piece // 4
        g_a = _sigmoid(_dot(h, win_ref[:, C_GA:C_GB]))
        cv = conv_upto(per_gate)
        ya = _ssm_readout(bu_s[rs, :].astype(_BF16), ua_s[rs, :], cmat_ref, d_ref, wglu_ref, bglu_ref)
        z_b = _silu(_dot(_after(h, cv), win_ref[:, C_ZB:C_Q]))
        cv = conv_upto(2 * per_gate)
        g_b = _sigmoid(_dot(_after(h, cv), win_ref[:, C_GB:C_GX]))
        cv = conv_upto(3 * per_gate)
        z_x = _silu(_dot(_after(h, cv), win_ref[:, C_ZX:C_GA]))
        cv = conv_upto(4 * per_gate)
        g_x = _sigmoid(_dot(_after(h, cv), win_ref[:, C_GX:C_END]))
        m = g_a * _dot((ya * za_s[rs, :]).astype(_BF16), wbra_ref[...])
        acc = jnp.concatenate(cvs, axis=0) + cb_ref[...]
        cb = _silu(_layernorm(acc, lng_ref[...], lnb_ref[...])) * z_b
        m = m + g_b * _dot(cb.astype(_BF16), wbrb_ref[...])
        ox = jnp.concatenate([o_s[hh, rs, :] for hh in range(N_HEADS)], axis=-1) * z_x
        m = m + g_x * _dot(ox.astype(_BF16), wbrx_ref[...])
        ts = pl.ds(pl.multiple_of(c * t_chunk, t_chunk), t_chunk)
        out = x_buf[slot, ts, :, :].reshape(chunk, D_MODEL) + _dot(m.astype(_BF16), wout_ref[...])
        if final:
            out = _rmsnorm(out, fg_ref[...])
        y_buf[slot, ts, :, :] = out.reshape(t_chunk, n_seq, D_MODEL)
        return carry

    lax.fori_loop(0, n_chunks, phase3, 0)

    for b in range(n_seq):
        out_copy(i, slot, b).start()

    new_tail = full_s[rows:rows + tail_rows, :]
    full_s[0:tail_rows, :] = new_tail

    @pl.when(i == n_tiles - 1)
    def _fin():
        tail_ref[...] = new_tail
        for b in range(n_seq):
            out_copy(i, slot, b).wait()
        if n_tiles >= 2:
            for b in range(n_seq):
                out_copy(i - 1, 1 - slot, b).wait()


_PROMPT_WEIGHTS = ('norm_g', 'w_in_bf', 'lam_re8', 'lam_im8', 'bmat_bf', 'cmat_bf', 'd', 'w_glu_bf', 'b_glu',
                   'w_br_ssm_bf', 'conv_w8', 'conv_b', 'ln_g', 'ln_b', 'w_br_conv_bf')


def _prompt_layer(x, p, layer, kb, va, final_g, *, rows, chunk, final):
    n_seq, seq, _ = x.shape
    n_t = rows // n_seq
    n_tiles = seq // n_t
    tail_rows = (CONV_K - 1) * n_seq
    kernel = functools.partial(_prompt_kernel, rows=rows, chunk=chunk, n_seq=n_seq, n_tiles=n_tiles, final=final)
    any_spec = pl.BlockSpec(memory_space=pl.ANY)
    stacked = [p[n] for n in _PROMPT_WEIGHTS] + [kb, va, p['w_br_xatt_bf'], p['w_out_bf']]
    state_spec = pl.BlockSpec((n_seq, N_SSM), lambda i: (0, 0))
    return pl.pallas_call(
        kernel,
        grid=(n_tiles,),
        in_specs=[any_spec] + [_layer_spec(a, layer) for a in stacked] + [_const_spec(final_g)],
        out_specs=[any_spec, state_spec, state_spec, pl.BlockSpec((tail_rows, E_BR), lambda i: (0, 0))],
        out_shape=[jax.ShapeDtypeStruct(x.shape, _F32),
                   jax.ShapeDtypeStruct((n_seq, N_SSM), _F32),
                   jax.ShapeDtypeStruct((n_seq, N_SSM), _F32),
                   jax.ShapeDtypeStruct((tail_rows, E_BR), _F32)],
        scratch_shapes=[pltpu.VMEM((2, n_t, n_seq, D_MODEL), _F32),
                        pltpu.VMEM((2, n_t, n_seq, D_MODEL), _F32),
                        pltpu.SemaphoreType.DMA((2, n_seq)),
                        pltpu.SemaphoreType.DMA((2, n_seq)),
                        pltpu.VMEM((rows, D_MODEL), _BF16),
                        pltpu.VMEM((rows, E_BR), _F32),
                        pltpu.VMEM((rows, E_BR), _F32),
                        pltpu.VMEM((rows, 2 * N_SSM), _F32),
                        pltpu.VMEM((tail_rows + rows, E_BR), _F32),
                        pltpu.VMEM((N_HEADS, rows, HEAD_DIM), _F32),
                        pltpu.VMEM((N_HEADS, rows, HEAD_DIM), _F32)],
        compiler_params=pltpu.CompilerParams(dimension_semantics=("arbitrary",), vmem_limit_bytes=VMEM_LIMIT),
        name="prompt_layer",
    )(x, *stacked, final_g)


def _sample_kernel(x_ref, ng_ref, win_ref, lre_ref, lim_ref, bmat_ref, cmat_ref, d_ref,
                   wglu_ref, bglu_ref, wbra_ref, cw_ref, cb_ref, lng_ref, lnb_ref, wbrb_ref,
                   s0re_ref, s0im_ref, cst_ref, k_ref, v_ref, wbrx_ref, wout_ref, fg_ref,
                   y_ref, sre_ref, sim_ref, cnew_ref,
                   x_s, h_s, m_s, q_s, o_s, vb_s, acc_s,
                   *, n, blk):
    layer = pl.program_id(0)
    i = pl.program_id(1)
    scale = 1.0 / math.sqrt(HEAD_DIM)
    groups = KV_ROWS // SUBLANES

    @pl.when((layer == 0) & (i == 0))
    def _load_x():
        x_s[...] = x_ref[...]

    @pl.when(i == 0)
    def _pre():
        h = _rmsnorm(x_s[...], ng_ref[...]).astype(_BF16)
        h_s[...] = h
        ua = _dot(h, win_ref[:, C_UA:C_ZA])
        ua_bf = ua.astype(_BF16)
        xs_parts = []
        for j in range(N_BLK):
            sl_s = slice(BLK_STATE * j, BLK_STATE * (j + 1))
            bu = _dot(ua_bf[:, LANES * j:LANES * (j + 1)], bmat_ref[j])
            l_re, l_im = lre_ref[:, sl_s], lim_ref[:, sl_s]
            o_re, o_im = s0re_ref[:, sl_s], s0im_ref[:, sl_s]
            n_re = l_re * o_re - l_im * o_im + bu[:, :BLK_STATE]
            n_im = l_re * o_im + l_im * o_re + bu[:, BLK_STATE:]
            sre_ref[:, sl_s] = n_re
            sim_ref[:, sl_s] = n_im
            xs_parts += [n_re, n_im]
        xs_bf = jnp.concatenate(xs_parts, axis=-1).astype(_BF16)
        ya = _ssm_readout(xs_bf, ua, cmat_ref, d_ref, wglu_ref, bglu_ref)
        ya = ya * _silu(_dot(h, win_ref[:, C_ZA:C_UB]))
        m_s[...] = _sigmoid(_dot(h, win_ref[:, C_GA:C_GB])) * _dot(ya.astype(_BF16), wbra_ref[...])
        ub = _dot(h, win_ref[:, C_UB:C_ZB])
        vb_s[...] = ub[:, :E_BR] * _sigmoid(ub[:, E_BR:])
        q = _dot(h, win_ref[:, C_Q:C_ZX]) * scale
        for j in range(SUBLANES):
            hh = j % N_HEADS
            q_s[n * j:n * (j + 1), :] = q[:, HEAD_DIM * hh:HEAD_DIM * (hh + 1)]

    ones = jnp.ones((HEAD_DIM, HEAD_DIM), _BF16)

    def fold(t):
        return t, pltpu.roll(t, N_HEADS, axis=0)

    for s in range(blk):
        seq = i * blk + s
        row = pl.ds(seq, 1)
        q8 = q_s[pl.ds(seq, SUBLANES, stride=n), :]
        kq = (k_ref[s].reshape(groups, SUBLANES, HEAD_DIM) * q8[None]).astype(_BF16)
        sc = _dot(kq.reshape(KV_ROWS, HEAD_DIM), ones).reshape(groups, SUBLANES, HEAD_DIM)
        a, b2 = fold(jnp.max(sc, axis=0))
        e = jnp.exp(sc - jnp.maximum(a, b2)[None])
        a, b2 = fold(jnp.sum(e, axis=0))
        den = a + b2
        a, b2 = fold(jnp.sum(e * v_ref[s].reshape(groups, SUBLANES, HEAD_DIM), axis=0))
        o8 = (a + b2) / den
        for hh in range(N_HEADS):
            o_s[hh, row, :] = o8[hh:hh + 1, :]
        acc_s[row, :] = jnp.sum(cst_ref[s] * cw_ref[0:CONV_K - 1, :], axis=0, keepdims=True)
        cnew_ref[s, 0:CONV_K - 2, :] = cst_ref[s, 1:CONV_K - 1, :]
        cnew_ref[s, CONV_K - 2:CONV_K - 1, :] = vb_s[row, :]

    @pl.when(i == pl.num_programs(1) - 1)
    def _post():
        h = h_s[...]
        acc = acc_s[...] + cb_ref[...] + cw_ref[CONV_K - 1:CONV_K, :] * vb_s[...]
        cb = _silu(_layernorm(acc, lng_ref[...], lnb_ref[...]))
        cb = cb * _silu(_dot(h, win_ref[:, C_ZB:C_Q]))
        m = m_s[...] + _sigmoid(_dot(h, win_ref[:, C_GB:C_GX])) * _dot(cb.astype(_BF16), wbrb_ref[...])
        ox = jnp.concatenate([o_s[hh] for hh in range(N_HEADS)], axis=-1)
        ox = ox * _silu(_dot(h, win_ref[:, C_ZX:C_GA]))
        m = m + _sigmoid(_dot(h, win_ref[:, C_GX:C_END])) * _dot(ox.astype(_BF16), wbrx_ref[...])
        out = x_s[...] + _dot(m.astype(_BF16), wout_ref[...])
        x_s[...] = out

        @pl.when(layer == pl.num_programs(0) - 1)
        def _final():
            y_ref[...] = _rmsnorm(out, fg_ref[...])


def _sample_trunk(x2d, p, s0re, s0im, cst, kc, vc, final_g, *, blk):
    n = x2d.shape[0]
    depth = cst.shape[0]
    kernel = functools.partial(_sample_kernel, n=n, blk=blk)

    def per_layer(arr):
        nd = arr.ndim - 1
        return pl.BlockSpec((None,) + arr.shape[1:], lambda l, i, _nd=nd: (l,) + (0,) * _nd,
                            pipeline_mode=pl.Buffered(1))

    blk_spec = lambda rest: pl.BlockSpec((None, blk) + rest, lambda l, i: (l, i, 0, 0))
    kv_spec = blk_spec((KV_ROWS, HEAD_DIM))
    win_spec = blk_spec((CONV_K - 1, E_BR))
    mid = [p[nm] for nm in ('lam_re1', 'lam_im1', 'bmat_bf', 'cmat_bf', 'd', 'w_glu_bf', 'b_glu', 'w_br_ssm_bf',
                            'conv_w', 'conv_b', 'ln_g', 'ln_b', 'w_br_conv_bf')]
    ins = [x2d, p['norm_g'], p['w_in_bf']] + mid + [s0re, s0im, cst, kc, vc, p['w_br_xatt_bf'], p['w_out_bf'],
                                                    final_g]
    in_specs = ([_const_spec(x2d), per_layer(p['norm_g']), per_layer(p['w_in_bf'])] + [per_layer(a) for a in mid]
                + [per_layer(s0re), per_layer(s0im), win_spec, kv_spec, kv_spec, per_layer(p['w_br_xatt_bf']),
                   per_layer(p['w_out_bf']), _const_spec(final_g)])
    state_spec = pl.BlockSpec((None, n, N_SSM), lambda l, i: (l, 0, 0))
    return pl.pallas_call(
        kernel,
        grid=(depth, n // blk),
        in_specs=in_specs,
        out_specs=[pl.BlockSpec((n, D_MODEL), lambda l, i: (0, 0)), state_spec, state_spec, win_spec],
        out_shape=[jax.ShapeDtypeStruct((n, D_MODEL), _F32),
                   jax.ShapeDtypeStruct((depth, n, N_SSM), _F32),
                   jax.ShapeDtypeStruct((depth, n, N_SSM), _F32),
                   jax.ShapeDtypeStruct(cst.shape, _F32)],
        scratch_shapes=[pltpu.VMEM((n, D_MODEL), _F32),
                        pltpu.VMEM((n, D_MODEL), _BF16),
                        pltpu.VMEM((n, D_MODEL), _F32),
                        pltpu.VMEM((SUBLANES * n, HEAD_DIM), _F32),
                        pltpu.VMEM((N_HEADS, n, HEAD_DIM), _F32),
                        pltpu.VMEM((n, E_BR), _F32),
                        pltpu.VMEM((n, E_BR), _F32)],
        compiler_params=pltpu.CompilerParams(dimension_semantics=("arbitrary", "arbitrary"),
                                             vmem_limit_bytes=VMEM_LIMIT),
        name="sample_trunk",
    )(*ins)


def _prep(norm_g, w_in, a_re, a_im, log_dt, b_re, b_im, c_re, c_im, d, w_glu, b_glu, w_br_ssm,
          conv_w, conv_b, ln_g, ln_b, w_br_conv, w_br_xatt, w_out):
    depth = norm_g.shape[0]
    dt = jnp.exp(log_dt)[..., None]
    mag = jnp.exp(a_re * dt)
    lr = mag * jnp.cos(a_im * dt)
    li = mag * jnp.sin(a_im * dt)
    den = a_re * a_re + a_im * a_im
    cr = ((lr - 1.0) * a_re + li * a_im) / den
    ci = (li * a_re - (lr - 1.0) * a_im) / den
    bb_re = cr[..., None] * b_re - ci[..., None] * b_im
    bb_im = cr[..., None] * b_im + ci[..., None] * b_re
    gpb = N_GROUPS // N_BLK
    eye = jnp.eye(gpb, dtype=_F32)

    def b_blocks(bb):
        bb = bb.reshape(depth, N_BLK, gpb, N_STATE, GROUP)
        return jnp.einsum('ljgph,gk->ljghkp', bb, eye).reshape(depth, N_BLK, LANES, BLK_STATE)

    def c_blocks(cc):
        cc = cc.reshape(depth, N_BLK, gpb, GROUP, N_STATE)
        return jnp.einsum('ljghp,gk->ljkpgh', cc, eye).reshape(depth, N_BLK, BLK_STATE, LANES)

    bmat = jnp.concatenate([b_blocks(bb_re), b_blocks(bb_im)], axis=-1)
    cmat = jnp.concatenate([c_blocks(c_re), -c_blocks(c_im)], axis=2)
    lr1 = lr.reshape(depth, 1, N_SSM)
    li1 = li.reshape(depth, 1, N_SSM)
    row = lambda v: v.reshape(depth, 1, -1)
    return {
        'norm_g': row(norm_g), 'w_in_bf': w_in.astype(_BF16),
        'lam_re1': lr1, 'lam_im1': li1,
        'lam_re8': jnp.broadcast_to(lr1, (depth, SUBLANES, N_SSM)),
        'lam_im8': jnp.broadcast_to(li1, (depth, SUBLANES, N_SSM)),
        'bmat_bf': bmat.astype(_BF16), 'cmat_bf': cmat.astype(_BF16),
        'd': row(d), 'w_glu_bf': w_glu.astype(_BF16), 'b_glu': row(b_glu),
        'w_br_ssm_bf': w_br_ssm.astype(_BF16), 'conv_w': conv_w, 'conv_w8': jnp.repeat(conv_w, SUBLANES, axis=1),
        'conv_b': row(conv_b),
        'ln_g': row(ln_g), 'ln_b': row(ln_b), 'w_br_conv_bf': w_br_conv.astype(_BF16),
        'w_br_xatt_bf': w_br_xatt.astype(_BF16), 'w_out_bf': w_out.astype(_BF16),
    }


def kernel(x_prompt, x_sample, mem_prompt, state_ssm_re, state_ssm_im, state_conv, cache_mem_k, cache_mem_v,
           norm_g, w_in, ssm_a_re, ssm_a_im, ssm_log_dt, ssm_b_re, ssm_b_im, ssm_c_re, ssm_c_im, ssm_d,
           w_glu, b_glu, w_br_ssm, conv_w, conv_b, conv_ln_g, conv_ln_b, w_br_conv,
           mem_norm_g, w_k, w_v, w_br_xatt, w_out, final_norm_g):
    bp, seq, _ = x_prompt.shape
    bs = x_sample.shape[0]
    depth = norm_g.shape[0]
    assert bp == SUBLANES and x_sample.shape[1] == 1 and depth == DEPTH
    rows, chunk, sample_blk = 512, 256, 8

    p = _prep(norm_g, w_in, ssm_a_re, ssm_a_im, ssm_log_dt, ssm_b_re, ssm_b_im, ssm_c_re, ssm_c_im, ssm_d,
              w_glu, b_glu, w_br_ssm, conv_w, conv_b, conv_ln_g, conv_ln_b, w_br_conv, w_br_xatt, w_out)
    final_g = final_norm_g.reshape(1, D_MODEL)
    wkv = jnp.concatenate([w_k, w_v], axis=-1).astype(_BF16)
    mem_k, mem_v, kb, va = _memkv(mem_prompt, mem_norm_g.reshape(depth, 1, D_MODEL), wkv)

    xp = x_prompt
    xs = x_sample.reshape(bs, D_MODEL)
    s0re = state_ssm_re.reshape(depth, bs, N_SSM)
    s0im = state_ssm_im.reshape(depth, bs, N_SSM)
    kc = cache_mem_k.reshape(depth, bs, KV_ROWS, HEAD_DIM)
    vc = cache_mem_v.reshape(depth, bs, KV_ROWS, HEAD_DIM)

    p_re, p_im, p_conv = [], [], []
    for l in range(depth):
        final = l == depth - 1
        xp, re, im, tail = _prompt_layer(xp, p, l, kb, va, final_g, rows=rows, chunk=chunk, final=final)
        p_re.append(re.reshape(bp, N_GROUPS, N_STATE))
        p_im.append(im.reshape(bp, N_GROUPS, N_STATE))
        p_conv.append(tail.reshape(CONV_K - 1, bp, E_BR).transpose(1, 0, 2))

    ys, s_re, s_im, s_conv = _sample_trunk(xs, p, s0re, s0im, state_conv, kc, vc, final_g, blk=sample_blk)

    y_prompt = xp
    y_sample = ys.reshape(bs, 1, D_MODEL)
    ssm_shape = (depth, bs, N_GROUPS, N_STATE)
    kv_shape = (depth, bp, N_MEM, N_HEADS, HEAD_DIM)
    return (y_prompt, y_sample, jnp.stack(p_re), jnp.stack(p_im), jnp.stack(p_conv), mem_k.reshape(kv_shape),
            mem_v.reshape(kv_shape), s_re.reshape(ssm_shape), s_im.reshape(ssm_shape), s_conv)
```

```python
import functools
import math

import jax
import jax.numpy as jnp
from jax import lax
from jax.experimental import pallas as pl
from jax.experimental.pallas import tpu as pltpu

D_MODEL = 1024
E_BR = 512
N_GROUPS = 32
GROUP = 16
N_STATE = 64
N_SSM = N_GROUPS * N_STATE
CONV_K = 31
N_HEADS = 4
HEAD_DIM = 128
N_MEM = 256
EPS = 1e-6
DEPTH = 2

LANES = 128
SUBLANES = 8
N_BLK = E_BR // LANES
BLK_STATE = N_SSM // N_BLK
KV_ROWS = N_MEM * N_HEADS

C_UA, C_ZA, C_UB, C_ZB, C_Q, C_ZX, C_GA, C_GB, C_GX, C_END = (
    0, 512, 1024, 2048, 2560, 3072, 3584, 4608, 5632, 6656)

VMEM_LIMIT = 60000 * 1024

_F32 = jnp.float32
_BF16 = jnp.bfloat16


def _dot(a, b):
    return jnp.dot(a, b, preferred_element_type=_F32)


def _sigmoid(x):
    return jax.nn.sigmoid(x)


def _silu(x):
    return x * _sigmoid(x)


def _rmsnorm(x, g):
    return x * lax.rsqrt(jnp.mean(x * x, axis=-1, keepdims=True) + EPS) * g


def _layernorm(x, g, b):
    mu = jnp.mean(x, axis=-1, keepdims=True)
    xc = x - mu
    var = jnp.mean(xc * xc, axis=-1, keepdims=True)
    return xc * lax.rsqrt(var + EPS) * g + b


def _layer_spec(arr, layer):
    nd = arr.ndim - 1
    return pl.BlockSpec((None,) + arr.shape[1:], lambda *_, _l=layer, _nd=nd: (_l,) + (0,) * _nd,
                        pipeline_mode=pl.Buffered(1))


def _const_spec(arr):
    nd = arr.ndim
    return pl.BlockSpec(arr.shape, lambda *_, _nd=nd: (0,) * _nd, pipeline_mode=pl.Buffered(1))


def _memkv_kernel(mem_ref, g_ref, wkv_ref, k_ref, v_ref, kb_ref, va_ref):
    hm = _rmsnorm(mem_ref[...], g_ref[...]).astype(_BF16)
    kv = _dot(hm, wkv_ref[...])
    kb_ref[...] = kv[:, :E_BR].astype(_BF16)
    ones = jnp.ones((N_MEM, HEAD_DIM), _BF16)
    for hh in range(N_HEADS):
        kh = kv[:, HEAD_DIM * hh:HEAD_DIM * (hh + 1)]
        vh = kv[:, E_BR + HEAD_DIM * hh:E_BR + HEAD_DIM * (hh + 1)]
        k_ref[pl.ds(hh, N_MEM, stride=N_HEADS), :] = kh
        v_ref[pl.ds(hh, N_MEM, stride=N_HEADS), :] = vh
        va_ref[hh, :, 0:HEAD_DIM] = vh.astype(_BF16)
        va_ref[hh, :, HEAD_DIM:2 * HEAD_DIM] = ones


def _memkv(mem, g, wkv):
    depth, bp = g.shape[0], mem.shape[0]
    kv_f = jax.ShapeDtypeStruct((depth, bp, KV_ROWS, HEAD_DIM), _F32)
    kv_spec = pl.BlockSpec((None, None, KV_ROWS, HEAD_DIM), lambda l, b: (l, b, 0, 0))
    return pl.pallas_call(
        _memkv_kernel,
        grid=(depth, bp),
        in_specs=[pl.BlockSpec((None, N_MEM, D_MODEL), lambda l, b: (b, 0, 0)),
                  pl.BlockSpec((None, 1, D_MODEL), lambda l, b: (l, 0, 0)),
                  pl.BlockSpec((None, D_MODEL, 2 * E_BR), lambda l, b: (l, 0, 0))],
        out_specs=[kv_spec, kv_spec,
                   pl.BlockSpec((None, None, N_MEM, E_BR), lambda l, b: (l, b, 0, 0)),
                   pl.BlockSpec((None, None, N_HEADS, N_MEM, 2 * HEAD_DIM), lambda l, b: (l, b, 0, 0, 0))],
        out_shape=[kv_f, kv_f,
                   jax.ShapeDtypeStruct((depth, bp, N_MEM, E_BR), _BF16),
                   jax.ShapeDtypeStruct((depth, bp, N_HEADS, N_MEM, 2 * HEAD_DIM), _BF16)],
        compiler_params=pltpu.CompilerParams(dimension_semantics=("arbitrary", "arbitrary")),
        name="mem_kv",
    )(mem, g, wkv)


def _ssm_readout(xs_bf, ua, cmat_ref, d_ref, wglu_ref, bglu_ref):
    ys = [_dot(xs_bf[:, 2 * BLK_STATE * j:2 * BLK_STATE * (j + 1)], cmat_ref[j]) for j in range(N_BLK)]
    y = jnp.concatenate(ys, axis=-1) + d_ref[...] * ua
    y = jax.nn.gelu(y, approximate=True)
    return y * _sigmoid(_dot(y.astype(_BF16), wglu_ref[...]) + bglu_ref[...])


def _conv_chunk(full_s, cw8_ref, r0, chunk, n_seq, piece=64):
    out = []
    for p in range(chunk // piece):
        acc = None
        for k in range(CONV_K):
            w = cw8_ref[SUBLANES * k:SUBLANES * (k + 1), :]
            win = full_s[pl.ds(r0 + p * piece + k * n_seq, piece), :].reshape(piece // SUBLANES, SUBLANES, E_BR)
            term = win * w[None]
            acc = term if acc is None else acc + term
        out.append(acc.reshape(piece, E_BR))
    return jnp.concatenate(out, axis=0)


def _prompt_kernel(x_hbm, ng_ref, win_ref, lre_ref, lim_ref, bmat_ref, cmat_ref, d_ref, wglu_ref, bglu_ref,
                   wbra_ref, cw8_ref, cb_ref, lng_ref, lnb_ref, wbrb_ref, k_ref, va_ref, wbrx_ref, wout_ref,
                   fg_ref,
                   y_hbm, sre_ref, sim_ref, tail_ref,
                   x_buf, y_buf, sem_in, sem_out, h_s, ua_s, bu_s, full_s, q_s, o_s,
                   *, rows, chunk, n_seq, n_tiles, final):
    i = pl.program_id(0)
    n_t = rows // n_seq
    tail_rows = (CONV_K - 1) * n_seq
    n_chunks = rows // chunk
    t_chunk = chunk // n_seq
    scale = 1.0 / math.sqrt(HEAD_DIM)
    slot = lax.rem(i, 2)

    def in_copy(tile, slt, b):
        return pltpu.make_async_copy(x_hbm.at[b, pl.ds(tile * n_t, n_t), :], x_buf.at[slt, :, b, :],
                                     sem_in.at[slt, b])

    def out_copy(tile, slt, b):
        return pltpu.make_async_copy(y_buf.at[slt, :, b, :], y_hbm.at[b, pl.ds(tile * n_t, n_t), :],
                                     sem_out.at[slt, b])

    @pl.when(i == 0)
    def _first_fetch():
        for b in range(n_seq):
            in_copy(0, 0, b).start()

    @pl.when(i + 1 < n_tiles)
    def _prefetch():
        for b in range(n_seq):
            in_copy(i + 1, 1 - slot, b).start()

    for b in range(n_seq):
        in_copy(i, slot, b).wait()

    @pl.when(i >= 2)
    def _drain_old_output():
        for b in range(n_seq):
            out_copy(i - 2, slot, b).wait()

    @pl.when(i == 0)
    def _init():
        sre_ref[...] = jnp.zeros_like(sre_ref)
        sim_ref[...] = jnp.zeros_like(sim_ref)
        full_s[0:tail_rows, :] = jnp.zeros((tail_rows, E_BR), _F32)

    def phase1(c, carry):
        r0 = pl.multiple_of(c * chunk, chunk)
        t0 = pl.multiple_of(c * t_chunk, t_chunk)
        x = x_buf[slot, pl.ds(t0, t_chunk), :, :].reshape(chunk, D_MODEL)
        h = _rmsnorm(x, ng_ref[...]).astype(_BF16)
        h_s[pl.ds(r0, chunk), :] = h
        ua = _dot(h, win_ref[:, C_UA:C_ZA])
        ua_s[pl.ds(r0, chunk), :] = ua
        ua_bf = ua.astype(_BF16)
        for j in range(N_BLK):
            bu_s[pl.ds(r0, chunk), 2 * BLK_STATE * j:2 * BLK_STATE * (j + 1)] = _dot(
                ua_bf[:, LANES * j:LANES * (j + 1)], bmat_ref[j])
        ub = _dot(h, win_ref[:, C_UB:C_ZB])
        full_s[pl.ds(tail_rows + r0, chunk), :] = ub[:, :E_BR] * _sigmoid(ub[:, E_BR:])
        q = _dot(h, win_ref[:, C_Q:C_ZX]) * scale
        for hh in range(N_HEADS):
            q_s[hh, pl.ds(r0, chunk), :] = q[:, HEAD_DIM * hh:HEAD_DIM * (hh + 1)]
        return carry

    lax.fori_loop(0, n_chunks, phase1, 0)

    blk_per_loop = 2
    for j0 in range(0, N_BLK, blk_per_loop):
        js = range(j0, j0 + blk_per_loop)
        lam = [(lre_ref[:, BLK_STATE * j:BLK_STATE * (j + 1)], lim_ref[:, BLK_STATE * j:BLK_STATE * (j + 1)])
               for j in js]

        def step(t, carry, js=js, lam=lam):
            r = pl.ds(pl.multiple_of(t * n_seq, n_seq), n_seq)
            new = []
            for (s_re, s_im), (lam_re, lam_im), j in zip(carry, lam, js):
                sl_re = slice(2 * BLK_STATE * j, 2 * BLK_STATE * j + BLK_STATE)
                sl_im = slice(2 * BLK_STATE * j + BLK_STATE, 2 * BLK_STATE * (j + 1))
                n_re = lam_re * s_re - lam_im * s_im + bu_s[r, sl_re]
                n_im = lam_re * s_im + lam_im * s_re + bu_s[r, sl_im]
                bu_s[r, sl_re] = n_re
                bu_s[r, sl_im] = n_im
                new.append((n_re, n_im))
            return tuple(new)

        init = tuple((sre_ref[:, BLK_STATE * j:BLK_STATE * (j + 1)], sim_ref[:, BLK_STATE * j:BLK_STATE * (j + 1)])
                     for j in js)
        fin = lax.fori_loop(0, n_t, step, init, unroll=2)
        for (s_re, s_im), j in zip(fin, js):
            sre_ref[:, BLK_STATE * j:BLK_STATE * (j + 1)] = s_re
            sim_ref[:, BLK_STATE * j:BLK_STATE * (j + 1)] = s_im

    pairs = [(b, hh) for b in range(n_seq) for hh in range(N_HEADS)]
    scores = []
    for b, hh in pairs:
        qb = q_s[hh, pl.ds(b, n_t, stride=n_seq), :].astype(_BF16)
        scores.append(lax.dot_general(qb, k_ref[b, :, HEAD_DIM * hh:HEAD_DIM * (hh + 1)],
                                      (((1,), (1,)), ((), ())), preferred_element_type=_F32))
    probs = [jnp.exp(s - jnp.max(s, axis=-1, keepdims=True)).astype(_BF16) for s in scores]
    outs = [_dot(p, va_ref[b, hh]) for p, (b, hh) in zip(probs, pairs)]
    for o, (b, hh) in zip(outs, pairs):
        o_s[hh, pl.ds(b, n_t, stride=n_seq), :] = o[:, :HEAD_DIM] / o[:, HEAD_DIM:]

    def phase3(c, carry):
        r0 = pl.multiple_of(c * chunk, chunk)
        rs = pl.ds(r0, chunk)
        h = h_s[rs, :]
        ya = _ssm_readout(bu_s[rs, :].astype(_BF16), ua_s[rs, :], cmat_ref, d_ref, wglu_ref, bglu_ref)
        ya = ya * _silu(_dot(h, win_ref[:, C_ZA:C_UB]))
        m = _sigmoid(_dot(h, win_ref[:, C_GA:C_GB])) * _dot(ya.astype(_BF16), wbra_ref[...])
        acc = _conv_chunk(full_s, cw8_ref, r0, chunk, n_seq) + cb_ref[...]
        cb = _silu(_layernorm(acc, lng_ref[...], lnb_ref[...]))
        cb = cb * _silu(_dot(h, win_ref[:, C_ZB:C_Q]))
        m = m + _sigmoid(_dot(h, win_ref[:, C_GB:C_GX])) * _dot(cb.astype(_BF16), wbrb_ref[...])
        ox = jnp.concatenate([o_s[hh, rs, :] for hh in range(N_HEADS)], axis=-1)
        ox = ox * _silu(_dot(h, win_ref[:, C_ZX:C_GA]))
        m = m + _sigmoid(_dot(h, win_ref[:, C_GX:C_END])) * _dot(ox.astype(_BF16), wbrx_ref[...])
        ts = pl.ds(pl.multiple_of(c * t_chunk, t_chunk), t_chunk)
        out = x_buf[slot, ts, :, :].reshape(chunk, D_MODEL) + _dot(m.astype(_BF16), wout_ref[...])
        if final:
            out = _rmsnorm(out, fg_ref[...])
        y_buf[slot, ts, :, :] = out.reshape(t_chunk, n_seq, D_MODEL)
        return carry

    lax.fori_loop(0, n_chunks, phase3, 0)

    for b in range(n_seq):
        out_copy(i, slot, b).start()

    new_tail = full_s[rows:rows + tail_rows, :]
    full_s[0:tail_rows, :] = new_tail

    @pl.when(i == n_tiles - 1)
    def _fin():
        tail_ref[...] = new_tail
        for b in range(n_seq):
            out_copy(i, slot, b).wait()
        if n_tiles >= 2:
            for b in range(n_seq):
                out_copy(i - 1, 1 - slot, b).wait()


_PROMPT_WEIGHTS = ('norm_g', 'w_in_bf', 'lam_re8', 'lam_im8', 'bmat_bf', 'cmat_bf', 'd', 'w_glu_bf', 'b_glu',
                   'w_br_ssm_bf', 'conv_w8', 'conv_b', 'ln_g', 'ln_b', 'w_br_conv_bf')


def _prompt_layer(x, p, layer, kb, va, final_g, *, rows, chunk, final):
    n_seq, seq, _ = x.shape
    n_t = rows // n_seq
    n_tiles = seq // n_t
    tail_rows = (CONV_K - 1) * n_seq
    kernel = functools.partial(_prompt_kernel, rows=rows, chunk=chunk, n_seq=n_seq, n_tiles=n_tiles, final=final)
    any_spec = pl.BlockSpec(memory_space=pl.ANY)
    stacked = [p[n] for n in _PROMPT_WEIGHTS] + [kb, va, p['w_br_xatt_bf'], p['w_out_bf']]
    state_spec = pl.BlockSpec((n_seq, N_SSM), lambda i: (0, 0))
    return pl.pallas_call(
        kernel,
        grid=(n_tiles,),
        in_specs=[any_spec] + [_layer_spec(a, layer) for a in stacked] + [_const_spec(final_g)],
        out_specs=[any_spec, state_spec, state_spec, pl.BlockSpec((tail_rows, E_BR), lambda i: (0, 0))],
        out_shape=[jax.ShapeDtypeStruct(x.shape, _F32),
                   jax.ShapeDtypeStruct((n_seq, N_SSM), _F32),
                   jax.ShapeDtypeStruct((n_seq, N_SSM), _F32),
                   jax.ShapeDtypeStruct((tail_rows, E_BR), _F32)],
        scratch_shapes=[pltpu.VMEM((2, n_t, n_seq, D_MODEL), _F32),
                        pltpu.VMEM((2, n_t, n_seq, D_MODEL), _F32),
                        pltpu.SemaphoreType.DMA((2, n_seq)),
                        pltpu.SemaphoreType.DMA((2, n_seq)),
                        pltpu.VMEM((rows, D_MODEL), _BF16),
                        pltpu.VMEM((rows, E_BR), _F32),
                        pltpu.VMEM((rows, 2 * N_SSM), _F32),
                        pltpu.VMEM((tail_rows + rows, E_BR), _F32),
                        pltpu.VMEM((N_HEADS, rows, HEAD_DIM), _F32),
                        pltpu.VMEM((N_HEADS, rows, HEAD_DIM), _F32)],
        compiler_params=pltpu.CompilerParams(dimension_semantics=("arbitrary",), vmem_limit_bytes=VMEM_LIMIT),
        name="prompt_layer",
    )(x, *stacked, final_g)


def _sample_kernel(x_ref, ng_ref, win_ref, lre_ref, lim_ref, bmat_ref, cmat_ref, d_ref,
                   wglu_ref, bglu_ref, wbra_ref, cw_ref, cb_ref, lng_ref, lnb_ref, wbrb_ref,
                   s0re_ref, s0im_ref, cst_ref, k_ref, v_ref, wbrx_ref, wout_ref, fg_ref,
                   y_ref, sre_ref, sim_ref, cnew_ref,
                   x_s, h_s, m_s, q_s, o_s, vb_s, acc_s,
                   *, n, blk):
    layer = pl.program_id(0)
    i = pl.program_id(1)
    scale = 1.0 / math.sqrt(HEAD_DIM)
    groups = KV_ROWS // SUBLANES

    @pl.when((layer == 0) & (i == 0))
    def _load_x():
        x_s[...] = x_ref[...]

    @pl.when(i == 0)
    def _pre():
        h = _rmsnorm(x_s[...], ng_ref[...]).astype(_BF16)
        h_s[...] = h
        ua = _dot(h, win_ref[:, C_UA:C_ZA])
        ua_bf = ua.astype(_BF16)
        xs_parts = []
        for j in range(N_BLK):
            sl_s = slice(BLK_STATE * j, BLK_STATE * (j + 1))
            bu = _dot(ua_bf[:, LANES * j:LANES * (j + 1)], bmat_ref[j])
            l_re, l_im = lre_ref[:, sl_s], lim_ref[:, sl_s]
            o_re, o_im = s0re_ref[:, sl_s], s0im_ref[:, sl_s]
            n_re = l_re * o_re - l_im * o_im + bu[:, :BLK_STATE]
            n_im = l_re * o_im + l_im * o_re + bu[:, BLK_STATE:]
            sre_ref[:, sl_s] = n_re
            sim_ref[:, sl_s] = n_im
            xs_parts += [n_re, n_im]
        xs_bf = jnp.concatenate(xs_parts, axis=-1).astype(_BF16)
        ya = _ssm_readout(xs_bf, ua, cmat_ref, d_ref, wglu_ref, bglu_ref)
        ya = ya * _silu(_dot(h, win_ref[:, C_ZA:C_UB]))
        m_s[...] = _sigmoid(_dot(h, win_ref[:, C_GA:C_GB])) * _dot(ya.astype(_BF16), wbra_ref[...])
        ub = _dot(h, win_ref[:, C_UB:C_ZB])
        vb_s[...] = ub[:, :E_BR] * _sigmoid(ub[:, E_BR:])
        q = _dot(h, win_ref[:, C_Q:C_ZX]) * scale
        for j in range(SUBLANES):
            hh = j % N_HEADS
            q_s[n * j:n * (j + 1), :] = q[:, HEAD_DIM * hh:HEAD_DIM * (hh + 1)]

    ones = jnp.ones((HEAD_DIM, HEAD_DIM), _BF16)

    def fold(t):
        return t, pltpu.roll(t, N_HEADS, axis=0)

    for s in range(blk):
        seq = i * blk + s
        row = pl.ds(seq, 1)
        q8 = q_s[pl.ds(seq, SUBLANES, stride=n), :]
        kq = (k_ref[s].reshape(groups, SUBLANES, HEAD_DIM) * q8[None]).astype(_BF16)
        sc = _dot(kq.reshape(KV_ROWS, HEAD_DIM), ones).reshape(groups, SUBLANES, HEAD_DIM)
        a, b2 = fold(jnp.max(sc, axis=0))
        e = jnp.exp(sc - jnp.maximum(a, b2)[None])
        a, b2 = fold(jnp.sum(e, axis=0))
        den = a + b2
        a, b2 = fold(jnp.sum(e * v_ref[s].reshape(groups, SUBLANES, HEAD_DIM), axis=0))
        o8 = (a + b2) / den
        for hh in range(N_HEADS):
            o_s[hh, row, :] = o8[hh:hh + 1, :]

    rows = pl.ds(pl.multiple_of(i * blk, blk), blk)
    acc = cst_ref[0] * cw_ref[0:1, :]
    for k in range(1, CONV_K - 1):
        acc = acc + cst_ref[k] * cw_ref[k:k + 1, :]
    acc_s[rows, :] = acc
    cnew_ref[0:CONV_K - 2] = cst_ref[1:CONV_K - 1]
    cnew_ref[CONV_K - 2] = vb_s[rows, :]

    @pl.when(i == pl.num_programs(1) - 1)
    def _post():
        h = h_s[...]
        acc = acc_s[...] + cb_ref[...] + cw_ref[CONV_K - 1:CONV_K, :] * vb_s[...]
        cb = _silu(_layernorm(acc, lng_ref[...], lnb_ref[...]))
        cb = cb * _silu(_dot(h, win_ref[:, C_ZB:C_Q]))
        m = m_s[...] + _sigmoid(_dot(h, win_ref[:, C_GB:C_GX])) * _dot(cb.astype(_BF16), wbrb_ref[...])
        ox = jnp.concatenate([o_s[hh] for hh in range(N_HEADS)], axis=-1)
        ox = ox * _silu(_dot(h, win_ref[:, C_ZX:C_GA]))
        m = m + _sigmoid(_dot(h, win_ref[:, C_GX:C_END])) * _dot(ox.astype(_BF16), wbrx_ref[...])
        out = x_s[...] + _dot(m.astype(_BF16), wout_ref[...])
        x_s[...] = out

        @pl.when(layer == pl.num_programs(0) - 1)
        def _final():
            y_ref[...] = _rmsnorm(out, fg_ref[...])


def _sample_trunk(x2d, p, s0re, s0im, cst, kc, vc, final_g, *, blk):
    n = x2d.shape[0]
    depth = cst.shape[0]
    kernel = functools.partial(_sample_kernel, n=n, blk=blk)

    def per_layer(arr):
        nd = arr.ndim - 1
        return pl.BlockSpec((None,) + arr.shape[1:], lambda l, i, _nd=nd: (l,) + (0,) * _nd,
                            pipeline_mode=pl.Buffered(1))

    kv_spec = pl.BlockSpec((None, blk, KV_ROWS, HEAD_DIM), lambda l, i: (l, i, 0, 0))
    win_spec = pl.BlockSpec((None, CONV_K - 1, blk, E_BR), lambda l, i: (l, 0, i, 0))
    mid = [p[nm] for nm in ('lam_re1', 'lam_im1', 'bmat_bf', 'cmat_bf', 'd', 'w_glu_bf', 'b_glu', 'w_br_ssm_bf',
                            'conv_w', 'conv_b', 'ln_g', 'ln_b', 'w_br_conv_bf')]
    ins = [x2d, p['norm_g'], p['w_in_bf']] + mid + [s0re, s0im, cst, kc, vc, p['w_br_xatt_bf'], p['w_out_bf'],
                                                    final_g]
    in_specs = ([_const_spec(x2d), per_layer(p['norm_g']), per_layer(p['w_in_bf'])] + [per_layer(a) for a in mid]
                + [per_layer(s0re), per_layer(s0im), win_spec, kv_spec, kv_spec, per_layer(p['w_br_xatt_bf']),
                   per_layer(p['w_out_bf']), _const_spec(final_g)])
    state_spec = pl.BlockSpec((None, n, N_SSM), lambda l, i: (l, 0, 0))
    return pl.pallas_call(
        kernel,
        grid=(depth, n // blk),
        in_specs=in_specs,
        out_specs=[pl.BlockSpec((n, D_MODEL), lambda l, i: (0, 0)), state_spec, state_spec, win_spec],
        out_shape=[jax.ShapeDtypeStruct((n, D_MODEL), _F32),
                   jax.ShapeDtypeStruct((depth, n, N_SSM), _F32),
                   jax.ShapeDtypeStruct((depth, n, N_SSM), _F32),
                   jax.ShapeDtypeStruct(cst.shape, _F32)],
        scratch_shapes=[pltpu.VMEM((n, D_MODEL), _F32),
                        pltpu.VMEM((n, D_MODEL), _BF16),
                        pltpu.VMEM((n, D_MODEL), _F32),
                        pltpu.VMEM((SUBLANES * n, HEAD_DIM), _F32),
                        pltpu.VMEM((N_HEADS, n, HEAD_DIM), _F32),
                        pltpu.VMEM((n, E_BR), _F32),
                        pltpu.VMEM((n, E_BR), _F32)],
        compiler_params=pltpu.CompilerParams(dimension_semantics=("arbitrary", "arbitrary"),
                                             vmem_limit_bytes=VMEM_LIMIT),
        name="sample_trunk",
    )(*ins)


def _prep(norm_g, w_in, a_re, a_im, log_dt, b_re, b_im, c_re, c_im, d, w_glu, b_glu, w_br_ssm,
          conv_w, conv_b, ln_g, ln_b, w_br_conv, w_br_xatt, w_out):
    depth = norm_g.shape[0]
    dt = jnp.exp(log_dt)[..., None]
    mag = jnp.exp(a_re * dt)
    lr = mag * jnp.cos(a_im * dt)
    li = mag * jnp.sin(a_im * dt)
    den = a_re * a_re + a_im * a_im
    cr = ((lr - 1.0) * a_re + li * a_im) / den
    ci = (li * a_re - (lr - 1.0) * a_im) / den
    bb_re = cr[..., None] * b_re - ci[..., None] * b_im
    bb_im = cr[..., None] * b_im + ci[..., None] * b_re
    gpb = N_GROUPS // N_BLK
    eye = jnp.eye(gpb, dtype=_F32)

    def b_blocks(bb):
        bb = bb.reshape(depth, N_BLK, gpb, N_STATE, GROUP)
        return jnp.einsum('ljgph,gk->ljghkp', bb, eye).reshape(depth, N_BLK, LANES, BLK_STATE)

    def c_blocks(cc):
        cc = cc.reshape(depth, N_BLK, gpb, GROUP, N_STATE)
        return jnp.einsum('ljghp,gk->ljkpgh', cc, eye).reshape(depth, N_BLK, BLK_STATE, LANES)

    bmat = jnp.concatenate([b_blocks(bb_re), b_blocks(bb_im)], axis=-1)
    cmat = jnp.concatenate([c_blocks(c_re), -c_blocks(c_im)], axis=2)
    lr1 = lr.reshape(depth, 1, N_SSM)
    li1 = li.reshape(depth, 1, N_SSM)
    row = lambda v: v.reshape(depth, 1, -1)
    return {
        'norm_g': row(norm_g), 'w_in_bf': w_in.astype(_BF16),
        'lam_re1': lr1, 'lam_im1': li1,
        'lam_re8': jnp.broadcast_to(lr1, (depth, SUBLANES, N_SSM)),
        'lam_im8': jnp.broadcast_to(li1, (depth, SUBLANES, N_SSM)),
        'bmat_bf': bmat.astype(_BF16), 'cmat_bf': cmat.astype(_BF16),
        'd': row(d), 'w_glu_bf': w_glu.astype(_BF16), 'b_glu': row(b_glu),
        'w_br_ssm_bf': w_br_ssm.astype(_BF16), 'conv_w': conv_w, 'conv_w8': jnp.repeat(conv_w, SUBLANES, axis=1),
        'conv_b': row(conv_b),
        'ln_g': row(ln_g), 'ln_b': row(ln_b), 'w_br_conv_bf': w_br_conv.astype(_BF16),
        'w_br_xatt_bf': w_br_xatt.astype(_BF16), 'w_out_bf': w_out.astype(_BF16),
    }


def kernel(x_prompt, x_sample, mem_prompt, state_ssm_re, state_ssm_im, state_conv, cache_mem_k, cache_mem_v,
           norm_g, w_in, ssm_a_re, ssm_a_im, ssm_log_dt, ssm_b_re, ssm_b_im, ssm_c_re, ssm_c_im, ssm_d,
           w_glu, b_glu, w_br_ssm, conv_w, conv_b, conv_ln_g, conv_ln_b, w_br_conv,
           mem_norm_g, w_k, w_v, w_br_xatt, w_out, final_norm_g):
    bp, seq, _ = x_prompt.shape
    bs = x_sample.shape[0]
    depth = norm_g.shape[0]
    assert bp == SUBLANES and x_sample.shape[1] == 1 and depth == DEPTH
    rows, chunk, sample_blk = 512, 256, 8

    p = _prep(norm_g, w_in, ssm_a_re, ssm_a_im, ssm_log_dt, ssm_b_re, ssm_b_im, ssm_c_re, ssm_c_im, ssm_d,
              w_glu, b_glu, w_br_ssm, conv_w, conv_b, conv_ln_g, conv_ln_b, w_br_conv, w_br_xatt, w_out)
    final_g = final_norm_g.reshape(1, D_MODEL)
    wkv = jnp.concatenate([w_k, w_v], axis=-1).astype(_BF16)
    mem_k, mem_v, kb, va = _memkv(mem_prompt, mem_norm_g.reshape(depth, 1, D_MODEL), wkv)

    xp = x_prompt
    xs = x_sample.reshape(bs, D_MODEL)
    s0re = state_ssm_re.reshape(depth, bs, N_SSM)
    s0im = state_ssm_im.reshape(depth, bs, N_SSM)
    kc = cache_mem_k.reshape(depth, bs, KV_ROWS, HEAD_DIM)
    vc = cache_mem_v.reshape(depth, bs, KV_ROWS, HEAD_DIM)

    p_re, p_im, p_conv = [], [], []
    for l in range(depth):
        final = l == depth - 1
        xp, re, im, tail = _prompt_layer(xp, p, l, kb, va, final_g, rows=rows, chunk=chunk, final=final)
        p_re.append(re.reshape(bp, N_GROUPS, N_STATE))
        p_im.append(im.reshape(bp, N_GROUPS, N_STATE))
        p_conv.append(tail.reshape(CONV_K - 1, bp, E_BR).transpose(1, 0, 2))

    ys, s_re, s_im, s_conv = _sample_trunk(xs, p, s0re, s0im, state_conv.transpose(0, 2, 1, 3), kc, vc, final_g,
                                           blk=sample_blk)
    s_conv = s_conv.transpose(0, 2, 1, 3)

    y_prompt = xp
    y_sample = ys.reshape(bs, 1, D_MODEL)
    ssm_shape = (depth, bs, N_GROUPS, N_STATE)
    kv_shape = (depth, bp, N_MEM, N_HEADS, HEAD_DIM)
    return (y_prompt, y_sample, jnp.stack(p_re), jnp.stack(p_im), jnp.stack(p_conv), mem_k.reshape(kv_shape),
            mem_v.reshape(kv_shape), s_re.reshape(ssm_shape), s_im.reshape(ssm_shape), s_conv)
```

```python
import functools
import math

import jax
import jax.numpy as jnp
from jax import lax
from jax.experimental import pallas as pl
from jax.experimental.pallas import tpu as pltpu

D_MODEL = 1024
E_BR = 512
N_GROUPS = 32
GROUP = 16
N_STATE = 64
N_SSM = N_GROUPS * N_STATE
CONV_K = 31
N_HEADS = 4
HEAD_DIM = 128
N_MEM = 256
EPS = 1e-6
DEPTH = 2

LANES = 128
SUBLANES = 8
N_BLK = E_BR // LANES
BLK_STATE = N_SSM // N_BLK
KV_ROWS = N_MEM * N_HEADS

C_UA, C_ZA, C_UB, C_ZB, C_Q, C_ZX, C_GA, C_GB, C_GX, C_END = (
    0, 512, 1024, 2048, 2560, 3072, 3584, 4608, 5632, 6656)

VMEM_LIMIT = 60000 * 1024

_F32 = jnp.float32
_BF16 = jnp.bfloat16


def _dot(a, b):
    return jnp.dot(a, b, preferred_element_type=_F32)


def _sigmoid(x):
    return jax.nn.sigmoid(x)


def _silu(x):
    return x * _sigmoid(x)


def _rmsnorm(x, g):
    return x * lax.rsqrt(jnp.mean(x * x, axis=-1, keepdims=True) + EPS) * g


def _layernorm(x, g, b):
    mu = jnp.mean(x, axis=-1, keepdims=True)
    xc = x - mu
    var = jnp.mean(xc * xc, axis=-1, keepdims=True)
    return xc * lax.rsqrt(var + EPS) * g + b


def _layer_spec(arr, layer):
    nd = arr.ndim - 1
    return pl.BlockSpec((None,) + arr.shape[1:], lambda *_, _l=layer, _nd=nd: (_l,) + (0,) * _nd,
                        pipeline_mode=pl.Buffered(1))


def _const_spec(arr):
    nd = arr.ndim
    return pl.BlockSpec(arr.shape, lambda *_, _nd=nd: (0,) * _nd, pipeline_mode=pl.Buffered(1))


def _memkv_kernel(mem_ref, g_ref, wkv_ref, k_ref, v_ref, kb_ref, va_ref, *, n_b):
    hm = _rmsnorm(mem_ref[...].reshape(n_b * N_MEM, D_MODEL), g_ref[...]).astype(_BF16)
    kv = _dot(hm, wkv_ref[...])
    ones = jnp.ones((N_MEM, HEAD_DIM), _BF16)
    for b in range(n_b):
        kvb = kv[N_MEM * b:N_MEM * (b + 1), :]
        kb_ref[b] = kvb[:, :E_BR].astype(_BF16)
        for hh in range(N_HEADS):
            kh = kvb[:, HEAD_DIM * hh:HEAD_DIM * (hh + 1)]
            vh = kvb[:, E_BR + HEAD_DIM * hh:E_BR + HEAD_DIM * (hh + 1)]
            k_ref[b, pl.ds(hh, N_MEM, stride=N_HEADS), :] = kh
            v_ref[b, pl.ds(hh, N_MEM, stride=N_HEADS), :] = vh
            va_ref[b, hh, :, 0:HEAD_DIM] = vh.astype(_BF16)
            va_ref[b, hh, :, HEAD_DIM:2 * HEAD_DIM] = ones


def _memkv(mem, g, wkv, n_b=4):
    depth, bp = g.shape[0], mem.shape[0]
    kv_f = jax.ShapeDtypeStruct((depth, bp, KV_ROWS, HEAD_DIM), _F32)
    kv_spec = pl.BlockSpec((None, n_b, KV_ROWS, HEAD_DIM), lambda l, b: (l, b, 0, 0))
    return pl.pallas_call(
        functools.partial(_memkv_kernel, n_b=n_b),
        grid=(depth, bp // n_b),
        in_specs=[pl.BlockSpec((n_b, N_MEM, D_MODEL), lambda l, b: (b, 0, 0)),
                  pl.BlockSpec((None, 1, D_MODEL), lambda l, b: (l, 0, 0)),
                  pl.BlockSpec((None, D_MODEL, 2 * E_BR), lambda l, b: (l, 0, 0))],
        out_specs=[kv_spec, kv_spec,
                   pl.BlockSpec((None, n_b, N_MEM, E_BR), lambda l, b: (l, b, 0, 0)),
                   pl.BlockSpec((None, n_b, N_HEADS, N_MEM, 2 * HEAD_DIM), lambda l, b: (l, b, 0, 0, 0))],
        out_shape=[kv_f, kv_f,
                   jax.ShapeDtypeStruct((depth, bp, N_MEM, E_BR), _BF16),
                   jax.ShapeDtypeStruct((depth, bp, N_HEADS, N_MEM, 2 * HEAD_DIM), _BF16)],
        compiler_params=pltpu.CompilerParams(dimension_semantics=("arbitrary", "arbitrary"),
                                             vmem_limit_bytes=VMEM_LIMIT),
        name="mem_kv",
    )(mem, g, wkv)


def _ssm_readout(xs_bf, ua, cmat_ref, d_ref, wglu_ref, bglu_ref):
    ys = [_dot(xs_bf[:, 2 * BLK_STATE * j:2 * BLK_STATE * (j + 1)], cmat_ref[j]) for j in range(N_BLK)]
    y = jnp.concatenate(ys, axis=-1) + d_ref[...] * ua
    y = jax.nn.gelu(y, approximate=True)
    return y * _sigmoid(_dot(y.astype(_BF16), wglu_ref[...]) + bglu_ref[...])


def _zero_of(*parts):
    acc = None
    for v in parts:
        for r in range(0, v.shape[0], SUBLANES):
            for c in range(0, v.shape[1], LANES):
                t = v[r:r + SUBLANES, c:c + LANES]
                acc = t if acc is None else acc + t
    return acc * 0.0


def _after(lhs, *parts):
    acc = _zero_of(*parts)
    zero = jnp.concatenate([acc, acc], axis=0).astype(_BF16)
    top = jnp.concatenate([lhs[0:2 * SUBLANES, 0:LANES] + zero, lhs[0:2 * SUBLANES, LANES:]], axis=1)
    return jnp.concatenate([top, lhs[2 * SUBLANES:]], axis=0)


def _conv_chunk(full_s, cw8_ref, r0, chunk, n_seq, piece=64):
    out = []
    for p in range(chunk // piece):
        acc = None
        for k in range(CONV_K):
            w = cw8_ref[SUBLANES * k:SUBLANES * (k + 1), :]
            win = full_s[pl.ds(r0 + p * piece + k * n_seq, piece), :].reshape(piece // SUBLANES, SUBLANES, E_BR)
            term = win * w[None]
            acc = term if acc is None else acc + term
        out.append(acc.reshape(piece, E_BR))
    return jnp.concatenate(out, axis=0)


def _prompt_kernel(x_hbm, ng_ref, win_ref, lre_ref, lim_ref, bmat_ref, cmat_ref, d_ref, wglu_ref, bglu_ref,
                   wbra_ref, cw8_ref, cb_ref, lng_ref, lnb_ref, wbrb_ref, k_ref, va_ref, wbrx_ref, wout_ref,
                   fg_ref,
                   y_hbm, sre_ref, sim_ref, tail_ref,
                   x_buf, y_buf, sem_in, sem_out, h_s, ua_s, za_s, bu_s, full_s, q_s, o_s,
                   *, rows, chunk, n_seq, n_tiles, final):
    i = pl.program_id(0)
    n_t = rows // n_seq
    tail_rows = (CONV_K - 1) * n_seq
    n_chunks = rows // chunk
    t_chunk = chunk // n_seq
    scale = 1.0 / math.sqrt(HEAD_DIM)
    slot = lax.rem(i, 2)

    def in_copy(tile, slt, b):
        return pltpu.make_async_copy(x_hbm.at[b, pl.ds(tile * n_t, n_t), :], x_buf.at[slt, :, b, :],
                                     sem_in.at[slt, b])

    def out_copy(tile, slt, b):
        return pltpu.make_async_copy(y_buf.at[slt, :, b, :], y_hbm.at[b, pl.ds(tile * n_t, n_t), :],
                                     sem_out.at[slt, b])

    @pl.when(i == 0)
    def _first_fetch():
        for b in range(n_seq):
            in_copy(0, 0, b).start()

    @pl.when(i + 1 < n_tiles)
    def _prefetch():
        for b in range(n_seq):
            in_copy(i + 1, 1 - slot, b).start()

    for b in range(n_seq):
        in_copy(i, slot, b).wait()

    @pl.when(i >= 2)
    def _drain_old_output():
        for b in range(n_seq):
            out_copy(i - 2, slot, b).wait()

    @pl.when(i == 0)
    def _init():
        sre_ref[...] = jnp.zeros_like(sre_ref)
        sim_ref[...] = jnp.zeros_like(sim_ref)
        full_s[0:tail_rows, :] = jnp.zeros((tail_rows, E_BR), _F32)

    def scan_block(j, bu, rs):
        sl_s = slice(BLK_STATE * j, BLK_STATE * (j + 1))
        lam_re, lam_im = lre_ref[:, sl_s], lim_ref[:, sl_s]
        s_re, s_im = sre_ref[:, sl_s], sim_ref[:, sl_s]
        res, ims = [], []
        for t in range(t_chunk):
            b_t = bu[n_seq * t:n_seq * (t + 1), :]
            s_re, s_im = (lam_re * s_re - lam_im * s_im + b_t[:, :BLK_STATE],
                          lam_re * s_im + lam_im * s_re + b_t[:, BLK_STATE:])
            res.append(s_re)
            ims.append(s_im)
        sre_ref[:, sl_s] = s_re
        sim_ref[:, sl_s] = s_im
        bu_s[rs, 2 * BLK_STATE * j:2 * BLK_STATE * j + BLK_STATE] = jnp.concatenate(res, axis=0)
        bu_s[rs, 2 * BLK_STATE * j + BLK_STATE:2 * BLK_STATE * (j + 1)] = jnp.concatenate(ims, axis=0)
        return s_re, s_im

    def phase1(c, carry):
        r0 = pl.multiple_of(c * chunk, chunk)
        rs = pl.ds(r0, chunk)
        t0 = pl.multiple_of(c * t_chunk, t_chunk)
        x = x_buf[slot, pl.ds(t0, t_chunk), :, :].reshape(chunk, D_MODEL)
        h = _rmsnorm(x, ng_ref[...]).astype(_BF16)
        h_s[rs, :] = h
        ua = _dot(h, win_ref[:, C_UA:C_ZA])
        ua_s[rs, :] = ua
        ua_bf = ua.astype(_BF16)
        drive = lambda j: _dot(ua_bf[:, LANES * j:LANES * (j + 1)], bmat_ref[j])
        b0, b1 = drive(0), drive(1)
        st = scan_block(0, b0, rs)
        q = _dot(_after(h, *st), win_ref[:, C_Q:C_ZX]) * scale
        for hh in range(N_HEADS):
            q_s[hh, rs, :] = q[:, HEAD_DIM * hh:HEAD_DIM * (hh + 1)]
        b2 = drive(2)
        st = scan_block(1, b1, rs)
        ub_lin = _dot(_after(h, *st), win_ref[:, C_UB:C_UB + E_BR])
        b3 = drive(3)
        st = scan_block(2, b2, rs)
        ub_gate = _dot(_after(h, *st), win_ref[:, C_UB + E_BR:C_ZB])
        full_s[pl.ds(tail_rows + r0, chunk), :] = ub_lin * _sigmoid(ub_gate)
        st = scan_block(3, b3, rs)
        za_s[rs, :] = _silu(_dot(_after(h, *st), win_ref[:, C_ZA:C_UB]))
        return carry

    lax.fori_loop(0, n_chunks, phase1, 0)

    pairs = [(b, hh) for b in range(n_seq) for hh in range(N_HEADS)]
    scores = []
    for b, hh in pairs:
        qb = q_s[hh, pl.ds(b, n_t, stride=n_seq), :].astype(_BF16)
        scores.append(lax.dot_general(qb, k_ref[b, :, HEAD_DIM * hh:HEAD_DIM * (hh + 1)],
                                      (((1,), (1,)), ((), ())), preferred_element_type=_F32))
    probs = [jnp.exp(s - jnp.max(s, axis=-1, keepdims=True)).astype(_BF16) for s in scores]
    outs = [_dot(p, va_ref[b, hh]) for p, (b, hh) in zip(probs, pairs)]
    for o, (b, hh) in zip(outs, pairs):
        o_s[hh, pl.ds(b, n_t, stride=n_seq), :] = o[:, :HEAD_DIM] / o[:, HEAD_DIM:]

    def phase3(c, carry):
        r0 = pl.multiple_of(c * chunk, chunk)
        rs = pl.ds(r0, chunk)
        h = h_s[rs, :]
        ya = _ssm_readout(bu_s[rs, :].astype(_BF16), ua_s[rs, :], cmat_ref, d_ref, wglu_ref, bglu_ref)
        ya = ya * za_s[rs, :]
        m = _sigmoid(_dot(h, win_ref[:, C_GA:C_GB])) * _dot(ya.astype(_BF16), wbra_ref[...])
        acc = _conv_chunk(full_s, cw8_ref, r0, chunk, n_seq) + cb_ref[...]
        cb = _silu(_layernorm(acc, lng_ref[...], lnb_ref[...]))
        cb = cb * _silu(_dot(h, win_ref[:, C_ZB:C_Q]))
        m = m + _sigmoid(_dot(h, win_ref[:, C_GB:C_GX])) * _dot(cb.astype(_BF16), wbrb_ref[...])
        ox = jnp.concatenate([o_s[hh, rs, :] for hh in range(N_HEADS)], axis=-1)
        ox = ox * _silu(_dot(h, win_ref[:, C_ZX:C_GA]))
        m = m + _sigmoid(_dot(h, win_ref[:, C_GX:C_END])) * _dot(ox.astype(_BF16), wbrx_ref[...])
        ts = pl.ds(pl.multiple_of(c * t_chunk, t_chunk), t_chunk)
        out = x_buf[slot, ts, :, :].reshape(chunk, D_MODEL) + _dot(m.astype(_BF16), wout_ref[...])
        if final:
            out = _rmsnorm(out, fg_ref[...])
        y_buf[slot, ts, :, :] = out.reshape(t_chunk, n_seq, D_MODEL)
        return carry

    lax.fori_loop(0, n_chunks, phase3, 0)

    for b in range(n_seq):
        out_copy(i, slot, b).start()

    new_tail = full_s[rows:rows + tail_rows, :]
    full_s[0:tail_rows, :] = new_tail

    @pl.when(i == n_tiles - 1)
    def _fin():
        tail_ref[...] = new_tail
        for b in range(n_seq):
            out_copy(i, slot, b).wait()
        if n_tiles >= 2:
            for b in range(n_seq):
                out_copy(i - 1, 1 - slot, b).wait()


_PROMPT_WEIGHTS = ('norm_g', 'w_in_bf', 'lam_re8', 'lam_im8', 'bmat_bf', 'cmat_bf', 'd', 'w_glu_bf', 'b_glu',
                   'w_br_ssm_bf', 'conv_w8', 'conv_b', 'ln_g', 'ln_b', 'w_br_conv_bf')


def _prompt_layer(x, p, layer, kb, va, final_g, *, rows, chunk, final):
    n_seq, seq, _ = x.shape
    n_t = rows // n_seq
    n_tiles = seq // n_t
    tail_rows = (CONV_K - 1) * n_seq
    kernel = functools.partial(_prompt_kernel, rows=rows, chunk=chunk, n_seq=n_seq, n_tiles=n_tiles, final=final)
    any_spec = pl.BlockSpec(memory_space=pl.ANY)
    stacked = [p[n] for n in _PROMPT_WEIGHTS] + [kb, va, p['w_br_xatt_bf'], p['w_out_bf']]
    state_spec = pl.BlockSpec((n_seq, N_SSM), lambda i: (0, 0))
    return pl.pallas_call(
        kernel,
        grid=(n_tiles,),
        in_specs=[any_spec] + [_layer_spec(a, layer) for a in stacked] + [_const_spec(final_g)],
        out_specs=[any_spec, state_spec, state_spec, pl.BlockSpec((tail_rows, E_BR), lambda i: (0, 0))],
        out_shape=[jax.ShapeDtypeStruct(x.shape, _F32),
                   jax.ShapeDtypeStruct((n_seq, N_SSM), _F32),
                   jax.ShapeDtypeStruct((n_seq, N_SSM), _F32),
                   jax.ShapeDtypeStruct((tail_rows, E_BR), _F32)],
        scratch_shapes=[pltpu.VMEM((2, n_t, n_seq, D_MODEL), _F32),
                        pltpu.VMEM((2, n_t, n_seq, D_MODEL), _F32),
                        pltpu.SemaphoreType.DMA((2, n_seq)),
                        pltpu.SemaphoreType.DMA((2, n_seq)),
                        pltpu.VMEM((rows, D_MODEL), _BF16),
                        pltpu.VMEM((rows, E_BR), _F32),
                        pltpu.VMEM((rows, E_BR), _F32),
                        pltpu.VMEM((rows, 2 * N_SSM), _F32),
                        pltpu.VMEM((tail_rows + rows, E_BR), _F32),
                        pltpu.VMEM((N_HEADS, rows, HEAD_DIM), _F32),
                        pltpu.VMEM((N_HEADS, rows, HEAD_DIM), _F32)],
        compiler_params=pltpu.CompilerParams(dimension_semantics=("arbitrary",), vmem_limit_bytes=VMEM_LIMIT),
        name="prompt_layer",
    )(x, *stacked, final_g)


def _sample_kernel(x_ref, ng_ref, win_ref, lre_ref, lim_ref, bmat_ref, cmat_ref, d_ref,
                   wglu_ref, bglu_ref, wbra_ref, cw_ref, cb_ref, lng_ref, lnb_ref, wbrb_ref,
                   s0re_ref, s0im_ref, cst_ref, k_ref, v_ref, wbrx_ref, wout_ref, fg_ref,
                   y_ref, sre_ref, sim_ref, cnew_ref,
                   x_s, h_s, m_s, q_s, o_s, vb_s, acc_s,
                   *, n, blk):
    layer = pl.program_id(0)
    i = pl.program_id(1)
    scale = 1.0 / math.sqrt(HEAD_DIM)
    groups = KV_ROWS // SUBLANES

    @pl.when((layer == 0) & (i == 0))
    def _load_x():
        x_s[...] = x_ref[...]

    @pl.when(i == 0)
    def _pre():
        h = _rmsnorm(x_s[...], ng_ref[...]).astype(_BF16)
        h_s[...] = h
        ua = _dot(h, win_ref[:, C_UA:C_ZA])
        ua_bf = ua.astype(_BF16)
        xs_parts = []
        for j in range(N_BLK):
            sl_s = slice(BLK_STATE * j, BLK_STATE * (j + 1))
            bu = _dot(ua_bf[:, LANES * j:LANES * (j + 1)], bmat_ref[j])
            l_re, l_im = lre_ref[:, sl_s], lim_ref[:, sl_s]
            o_re, o_im = s0re_ref[:, sl_s], s0im_ref[:, sl_s]
            n_re = l_re * o_re - l_im * o_im + bu[:, :BLK_STATE]
            n_im = l_re * o_im + l_im * o_re + bu[:, BLK_STATE:]
            sre_ref[:, sl_s] = n_re
            sim_ref[:, sl_s] = n_im
            xs_parts += [n_re, n_im]
        xs_bf = jnp.concatenate(xs_parts, axis=-1).astype(_BF16)
        ya = _ssm_readout(xs_bf, ua, cmat_ref, d_ref, wglu_ref, bglu_ref)
        ya = ya * _silu(_dot(h, win_ref[:, C_ZA:C_UB]))
        m_s[...] = _sigmoid(_dot(h, win_ref[:, C_GA:C_GB])) * _dot(ya.astype(_BF16), wbra_ref[...])
        ub = _dot(h, win_ref[:, C_UB:C_ZB])
        vb_s[...] = ub[:, :E_BR] * _sigmoid(ub[:, E_BR:])
        q = _dot(h, win_ref[:, C_Q:C_ZX]) * scale
        for j in range(SUBLANES):
            hh = j % N_HEADS
            q_s[n * j:n * (j + 1), :] = q[:, HEAD_DIM * hh:HEAD_DIM * (hh + 1)]

    ones = jnp.ones((HEAD_DIM, HEAD_DIM), _BF16)

    def fold(t):
        return t, pltpu.roll(t, N_HEADS, axis=0)

    for s in range(blk):
        seq = i * blk + s
        row = pl.ds(seq, 1)
        q8 = q_s[pl.ds(seq, SUBLANES, stride=n), :]
        kq = (k_ref[s].reshape(groups, SUBLANES, HEAD_DIM) * q8[None]).astype(_BF16)
        sc = _dot(kq.reshape(KV_ROWS, HEAD_DIM), ones).reshape(groups, SUBLANES, HEAD_DIM)
        a, b2 = fold(jnp.max(sc, axis=0))
        e = jnp.exp(sc - jnp.maximum(a, b2)[None])
        a, b2 = fold(jnp.sum(e, axis=0))
        den = a + b2
        a, b2 = fold(jnp.sum(e * v_ref[s].reshape(groups, SUBLANES, HEAD_DIM), axis=0))
        o8 = (a + b2) / den
        for hh in range(N_HEADS):
            o_s[hh, row, :] = o8[hh:hh + 1, :]

    rows = pl.ds(pl.multiple_of(i * blk, blk), blk)
    acc = cst_ref[0] * cw_ref[0:1, :]
    for k in range(1, CONV_K - 1):
        acc = acc + cst_ref[k] * cw_ref[k:k + 1, :]
    acc_s[rows, :] = acc
    cnew_ref[0:CONV_K - 2] = cst_ref[1:CONV_K - 1]
    cnew_ref[CONV_K - 2] = vb_s[rows, :]

    @pl.when(i == pl.num_programs(1) - 1)
    def _post():
        h = h_s[...]
        acc = acc_s[...] + cb_ref[...] + cw_ref[CONV_K - 1:CONV_K, :] * vb_s[...]
        cb = _silu(_layernorm(acc, lng_ref[...], lnb_ref[...]))
        cb = cb * _silu(_dot(h, win_ref[:, C_ZB:C_Q]))
        m = m_s[...] + _sigmoid(_dot(h, win_ref[:, C_GB:C_GX])) * _dot(cb.astype(_BF16), wbrb_ref[...])
        ox = jnp.concatenate([o_s[hh] for hh in range(N_HEADS)], axis=-1)
        ox = ox * _silu(_dot(h, win_ref[:, C_ZX:C_GA]))
        m = m + _sigmoid(_dot(h, win_ref[:, C_GX:C_END])) * _dot(ox.astype(_BF16), wbrx_ref[...])
        out = x_s[...] + _dot(m.astype(_BF16), wout_ref[...])
        x_s[...] = out

        @pl.when(layer == pl.num_programs(0) - 1)
        def _final():
            y_ref[...] = _rmsnorm(out, fg_ref[...])


def _sample_trunk(x2d, p, s0re, s0im, cst, kc, vc, final_g, *, blk):
    n = x2d.shape[0]
    depth = cst.shape[0]
    kernel = functools.partial(_sample_kernel, n=n, blk=blk)

    def per_layer(arr):
        nd = arr.ndim - 1
        return pl.BlockSpec((None,) + arr.shape[1:], lambda l, i, _nd=nd: (l,) + (0,) * _nd,
                            pipeline_mode=pl.Buffered(1))

    kv_spec = pl.BlockSpec((None, blk, KV_ROWS, HEAD_DIM), lambda l, i: (l, i, 0, 0))
    win_spec = pl.BlockSpec((None, CONV_K - 1, blk, E_BR), lambda l, i: (l, 0, i, 0))
    mid = [p[nm] for nm in ('lam_re1', 'lam_im1', 'bmat_bf', 'cmat_bf', 'd', 'w_glu_bf', 'b_glu', 'w_br_ssm_bf',
                            'conv_w', 'conv_b', 'ln_g', 'ln_b', 'w_br_conv_bf')]
    ins = [x2d, p['norm_g'], p['w_in_bf']] + mid + [s0re, s0im, cst, kc, vc, p['w_br_xatt_bf'], p['w_out_bf'],
                                                    final_g]
    in_specs = ([_const_spec(x2d), per_layer(p['norm_g']), per_layer(p['w_in_bf'])] + [per_layer(a) for a in mid]
                + [per_layer(s0re), per_layer(s0im), win_spec, kv_spec, kv_spec, per_layer(p['w_br_xatt_bf']),
                   per_layer(p['w_out_bf']), _const_spec(final_g)])
    state_spec = pl.BlockSpec((None, n, N_SSM), lambda l, i: (l, 0, 0))
    return pl.pallas_call(
        kernel,
        grid=(depth, n // blk),
        in_specs=in_specs,
        out_specs=[pl.BlockSpec((n, D_MODEL), lambda l, i: (0, 0)), state_spec, state_spec, win_spec],
        out_shape=[jax.ShapeDtypeStruct((n, D_MODEL), _F32),
                   jax.ShapeDtypeStruct((depth, n, N_SSM), _F32),
                   jax.ShapeDtypeStruct((depth, n, N_SSM), _F32),
                   jax.ShapeDtypeStruct(cst.shape, _F32)],
        scratch_shapes=[pltpu.VMEM((n, D_MODEL), _F32),
                        pltpu.VMEM((n, D_MODEL), _BF16),
                        pltpu.VMEM((n, D_MODEL), _F32),
                        pltpu.VMEM((SUBLANES * n, HEAD_DIM), _F32),
                        pltpu.VMEM((N_HEADS, n, HEAD_DIM), _F32),
                        pltpu.VMEM((n, E_BR), _F32),
                        pltpu.VMEM((n, E_BR), _F32)],
        compiler_params=pltpu.CompilerParams(dimension_semantics=("arbitrary", "arbitrary"),
                                             vmem_limit_bytes=VMEM_LIMIT),
        name="sample_trunk",
    )(*ins)


def _prep(norm_g, w_in, a_re, a_im, log_dt, b_re, b_im, c_re, c_im, d, w_glu, b_glu, w_br_ssm,
          conv_w, conv_b, ln_g, ln_b, w_br_conv, w_br_xatt, w_out):
    depth = norm_g.shape[0]
    dt = jnp.exp(log_dt)[..., None]
    mag = jnp.exp(a_re * dt)
    lr = mag * jnp.cos(a_im * dt)
    li = mag * jnp.sin(a_im * dt)
    den = a_re * a_re + a_im * a_im
    cr = ((lr - 1.0) * a_re + li * a_im) / den
    ci = (li * a_re - (lr - 1.0) * a_im) / den
    bb_re = cr[..., None] * b_re - ci[..., None] * b_im
    bb_im = cr[..., None] * b_im + ci[..., None] * b_re
    gpb = N_GROUPS // N_BLK
    eye = jnp.eye(gpb, dtype=_F32)

    def b_blocks(bb):
        bb = bb.reshape(depth, N_BLK, gpb, N_STATE, GROUP)
        return jnp.einsum('ljgph,gk->ljghkp', bb, eye).reshape(depth, N_BLK, LANES, BLK_STATE)

    def c_blocks(cc):
        cc = cc.reshape(depth, N_BLK, gpb, GROUP, N_STATE)
        return jnp.einsum('ljghp,gk->ljkpgh', cc, eye).reshape(depth, N_BLK, BLK_STATE, LANES)

    bmat = jnp.concatenate([b_blocks(bb_re), b_blocks(bb_im)], axis=-1)
    cmat = jnp.concatenate([c_blocks(c_re), -c_blocks(c_im)], axis=2)
    lr1 = lr.reshape(depth, 1, N_SSM)
    li1 = li.reshape(depth, 1, N_SSM)
    row = lambda v: v.reshape(depth, 1, -1)
    return {
        'norm_g': row(norm_g), 'w_in_bf': w_in.astype(_BF16),
        'lam_re1': lr1, 'lam_im1': li1,
        'lam_re8': jnp.broadcast_to(lr1, (depth, SUBLANES, N_SSM)),
        'lam_im8': jnp.broadcast_to(li1, (depth, SUBLANES, N_SSM)),
        'bmat_bf': bmat.astype(_BF16), 'cmat_bf': cmat.astype(_BF16),
        'd': row(d), 'w_glu_bf': w_glu.astype(_BF16), 'b_glu': row(b_glu),
        'w_br_ssm_bf': w_br_ssm.astype(_BF16), 'conv_w': conv_w, 'conv_w8': jnp.repeat(conv_w, SUBLANES, axis=1),
        'conv_b': row(conv_b),
        'ln_g': row(ln_g), 'ln_b': row(ln_b), 'w_br_conv_bf': w_br_conv.astype(_BF16),
        'w_br_xatt_bf': w_br_xatt.astype(_BF16), 'w_out_bf': w_out.astype(_BF16),
    }


def kernel(x_prompt, x_sample, mem_prompt, state_ssm_re, state_ssm_im, state_conv, cache_mem_k, cache_mem_v,
           norm_g, w_in, ssm_a_re, ssm_a_im, ssm_log_dt, ssm_b_re, ssm_b_im, ssm_c_re, ssm_c_im, ssm_d,
           w_glu, b_glu, w_br_ssm, conv_w, conv_b, conv_ln_g, conv_ln_b, w_br_conv,
           mem_norm_g, w_k, w_v, w_br_xatt, w_out, final_norm_g):
    bp, seq, _ = x_prompt.shape
    bs = x_sample.shape[0]
    depth = norm_g.shape[0]
    assert bp == SUBLANES and x_sample.shape[1] == 1 and depth == DEPTH
    rows, chunk, sample_blk = 512, 256, 8

    p = _prep(norm_g, w_in, ssm_a_re, ssm_a_im, ssm_log_dt, ssm_b_re, ssm_b_im, ssm_c_re, ssm_c_im, ssm_d,
              w_glu, b_glu, w_br_ssm, conv_w, conv_b, conv_ln_g, conv_ln_b, w_br_conv, w_br_xatt, w_out)
    final_g = final_norm_g.reshape(1, D_MODEL)
    wkv = jnp.concatenate([w_k, w_v], axis=-1).astype(_BF16)
    mem_k, mem_v, kb, va = _memkv(mem_prompt, mem_norm_g.reshape(depth, 1, D_MODEL), wkv)

    xp = x_prompt
    xs = x_sample.reshape(bs, D_MODEL)
    s0re = state_ssm_re.reshape(depth, bs, N_SSM)
    s0im = state_ssm_im.reshape(depth, bs, N_SSM)
    kc = cache_mem_k.reshape(depth, bs, KV_ROWS, HEAD_DIM)
    vc = cache_mem_v.reshape(depth, bs, KV_ROWS, HEAD_DIM)

    p_re, p_im, p_conv = [], [], []
    for l in range(depth):
        final = l == depth - 1
        xp, re, im, tail = _prompt_layer(xp, p, l, kb, va, final_g, rows=rows, chunk=chunk, final=final)
        p_re.append(re.reshape(bp, N_GROUPS, N_STATE))
        p_im.append(im.reshape(bp, N_GROUPS, N_STATE))
        p_conv.append(tail.reshape(CONV_K - 1, bp, E_BR).transpose(1, 0, 2))

    ys, s_re, s_im, s_conv = _sample_trunk(xs, p, s0re, s0im, state_conv.transpose(0, 2, 1, 3), kc, vc, final_g,
                                           blk=sample_blk)
    s_conv = s_conv.transpose(0, 2, 1, 3)

    y_prompt = xp
    y_sample = ys.reshape(bs, 1, D_MODEL)
    ssm_shape = (depth, bs, N_GROUPS, N_STATE)
    kv_shape = (depth, bp, N_MEM, N_HEADS, HEAD_DIM)
    return (y_prompt, y_sample, jnp.stack(p_re), jnp.stack(p_im), jnp.stack(p_conv), mem_k.reshape(kv_shape),
            mem_v.reshape(kv_shape), s_re.reshape(ssm_shape), s_im.reshape(ssm_shape), s_conv)
```

```python
import functools
import math

import jax
import jax.numpy as jnp
from jax import lax
from jax.experimental import pallas as pl
from jax.experimental.pallas import tpu as pltpu

D_MODEL = 1024
E_BR = 512
N_GROUPS = 32
GROUP = 16
N_STATE = 64
N_SSM = N_GROUPS * N_STATE
CONV_K = 31
N_HEADS = 4
HEAD_DIM = 128
N_MEM = 256
EPS = 1e-6
DEPTH = 2

LANES = 128
SUBLANES = 8
N_BLK = E_BR // LANES
BLK_STATE = N_SSM // N_BLK
KV_ROWS = N_MEM * N_HEADS

C_UA, C_ZA, C_UB, C_ZB, C_Q, C_ZX, C_GA, C_GB, C_GX, C_END = (
    0, 512, 1024, 2048, 2560, 3072, 3584, 4608, 5632, 6656)

VMEM_LIMIT = 60000 * 1024

PROMPT_TILE_ROWS = 512
PROMPT_CHUNK_ROWS = 256
CONV_PIECE_ROWS = 64
SAMPLE_BLOCK = 8
MEMKV_BLOCK = 8

_F32 = jnp.float32
_BF16 = jnp.bfloat16


def _dot(a, b):
    return jnp.dot(a, b, preferred_element_type=_F32)


def _sigmoid(x):
    return jax.nn.sigmoid(x)


def _silu(x):
    return x * _sigmoid(x)


def _rmsnorm(x, g):
    return x * lax.rsqrt(jnp.mean(x * x, axis=-1, keepdims=True) + EPS) * g


def _layernorm(x, g, b):
    mu = jnp.mean(x, axis=-1, keepdims=True)
    xc = x - mu
    var = jnp.mean(xc * xc, axis=-1, keepdims=True)
    return xc * lax.rsqrt(var + EPS) * g + b


def _layer_spec(arr, layer):
    nd = arr.ndim - 1
    return pl.BlockSpec((None,) + arr.shape[1:], lambda *_, _l=layer, _nd=nd: (_l,) + (0,) * _nd,
                        pipeline_mode=pl.Buffered(1))


def _const_spec(arr):
    nd = arr.ndim
    return pl.BlockSpec(arr.shape, lambda *_, _nd=nd: (0,) * _nd, pipeline_mode=pl.Buffered(1))


def _memkv_kernel(mem_ref, g_ref, wkv_ref, k_ref, v_ref, kb_ref, va_ref, *, n_b):
    hm = _rmsnorm(mem_ref[...].reshape(n_b * N_MEM, D_MODEL), g_ref[...]).astype(_BF16)
    kv = _dot(hm, wkv_ref[...])
    ones = jnp.ones((N_MEM, HEAD_DIM), _BF16)
    for b in range(n_b):
        kvb = kv[N_MEM * b:N_MEM * (b + 1), :]
        kb_ref[b] = kvb[:, :E_BR].astype(_BF16)
        for hh in range(N_HEADS):
            kh = kvb[:, HEAD_DIM * hh:HEAD_DIM * (hh + 1)]
            vh = kvb[:, E_BR + HEAD_DIM * hh:E_BR + HEAD_DIM * (hh + 1)]
            k_ref[b, pl.ds(hh, N_MEM, stride=N_HEADS), :] = kh
            v_ref[b, pl.ds(hh, N_MEM, stride=N_HEADS), :] = vh
            va_ref[b, hh, :, 0:HEAD_DIM] = vh.astype(_BF16)
            va_ref[b, hh, :, HEAD_DIM:2 * HEAD_DIM] = ones


def _memkv(mem, g, wkv, n_b=MEMKV_BLOCK):
    depth, bp = g.shape[0], mem.shape[0]
    kv_f = jax.ShapeDtypeStruct((depth, bp, KV_ROWS, HEAD_DIM), _F32)
    kv_spec = pl.BlockSpec((None, n_b, KV_ROWS, HEAD_DIM), lambda l, b: (l, b, 0, 0))
    return pl.pallas_call(
        functools.partial(_memkv_kernel, n_b=n_b),
        grid=(depth, bp // n_b),
        in_specs=[pl.BlockSpec((n_b, N_MEM, D_MODEL), lambda l, b: (b, 0, 0)),
                  pl.BlockSpec((None, 1, D_MODEL), lambda l, b: (l, 0, 0)),
                  pl.BlockSpec((None, D_MODEL, 2 * E_BR), lambda l, b: (l, 0, 0))],
        out_specs=[kv_spec, kv_spec,
                   pl.BlockSpec((None, n_b, N_MEM, E_BR), lambda l, b: (l, b, 0, 0)),
                   pl.BlockSpec((None, n_b, N_HEADS, N_MEM, 2 * HEAD_DIM), lambda l, b: (l, b, 0, 0, 0))],
        out_shape=[kv_f, kv_f,
                   jax.ShapeDtypeStruct((depth, bp, N_MEM, E_BR), _BF16),
                   jax.ShapeDtypeStruct((depth, bp, N_HEADS, N_MEM, 2 * HEAD_DIM), _BF16)],
        compiler_params=pltpu.CompilerParams(dimension_semantics=("arbitrary", "arbitrary"),
                                             vmem_limit_bytes=VMEM_LIMIT),
        name="mem_kv",
    )(mem, g, wkv)


def _ssm_readout(xs_bf, ua, cmat_ref, d_ref, wglu_ref, bglu_ref):
    ys = [_dot(xs_bf[:, 2 * BLK_STATE * j:2 * BLK_STATE * (j + 1)], cmat_ref[j]) for j in range(N_BLK)]
    y = jnp.concatenate(ys, axis=-1) + d_ref[...] * ua
    y = jax.nn.gelu(y, approximate=True)
    return y * _sigmoid(_dot(y.astype(_BF16), wglu_ref[...]) + bglu_ref[...])


def _zero_of(*parts):
    acc = None
    for v in parts:
        for r in range(0, v.shape[0], SUBLANES):
            for c in range(0, v.shape[1], LANES):
                t = v[r:r + SUBLANES, c:c + LANES]
                acc = t if acc is None else acc + t
    return acc * 0.0


def _after(lhs, *parts):
    acc = _zero_of(*parts)
    zero = jnp.concatenate([acc, acc], axis=0).astype(_BF16)
    top = jnp.concatenate([lhs[0:2 * SUBLANES, 0:LANES] + zero, lhs[0:2 * SUBLANES, LANES:]], axis=1)
    return jnp.concatenate([top, lhs[2 * SUBLANES:]], axis=0)


def _conv_chunk(full_s, cw8_ref, r0, chunk, n_seq, piece=CONV_PIECE_ROWS):
    out = []
    for p in range(chunk // piece):
        acc = None
        for k in range(CONV_K):
            w = cw8_ref[SUBLANES * k:SUBLANES * (k + 1), :]
            win = full_s[pl.ds(r0 + p * piece + k * n_seq, piece), :].reshape(piece // SUBLANES, SUBLANES, E_BR)
            term = win * w[None]
            acc = term if acc is None else acc + term
        out.append(acc.reshape(piece, E_BR))
    return jnp.concatenate(out, axis=0)


def _prompt_kernel(x_hbm, ng_ref, win_ref, lre_ref, lim_ref, bmat_ref, cmat_ref, d_ref, wglu_ref, bglu_ref,
                   wbra_ref, cw8_ref, cb_ref, lng_ref, lnb_ref, wbrb_ref, k_ref, va_ref, wbrx_ref, wout_ref,
                   fg_ref,
                   y_hbm, sre_ref, sim_ref, tail_ref,
                   x_buf, y_buf, sem_in, sem_out, h_s, ua_s, za_s, bu_s, full_s, q_s, o_s,
                   *, rows, chunk, n_seq, n_tiles, final):
    i = pl.program_id(0)
    n_t = rows // n_seq
    tail_rows = (CONV_K - 1) * n_seq
    n_chunks = rows // chunk
    t_chunk = chunk // n_seq
    scale = 1.0 / math.sqrt(HEAD_DIM)
    slot = lax.rem(i, 2)

    def in_copy(tile, slt, b):
        return pltpu.make_async_copy(x_hbm.at[b, pl.ds(tile * n_t, n_t), :], x_buf.at[slt, :, b, :],
                                     sem_in.at[slt, b])

    def out_copy(tile, slt, b):
        return pltpu.make_async_copy(y_buf.at[slt, :, b, :], y_hbm.at[b, pl.ds(tile * n_t, n_t), :],
                                     sem_out.at[slt, b])

    @pl.when(i == 0)
    def _first_fetch():
        for b in range(n_seq):
            in_copy(0, 0, b).start()

    @pl.when(i + 1 < n_tiles)
    def _prefetch():
        for b in range(n_seq):
            in_copy(i + 1, 1 - slot, b).start()

    for b in range(n_seq):
        in_copy(i, slot, b).wait()

    @pl.when(i >= 2)
    def _drain_old_output():
        for b in range(n_seq):
            out_copy(i - 2, slot, b).wait()

    @pl.when(i == 0)
    def _init():
        sre_ref[...] = jnp.zeros_like(sre_ref)
        sim_ref[...] = jnp.zeros_like(sim_ref)
        full_s[0:tail_rows, :] = jnp.zeros((tail_rows, E_BR), _F32)

    def scan_block(j, bu, rs):
        sl_s = slice(BLK_STATE * j, BLK_STATE * (j + 1))
        lam_re, lam_im = lre_ref[:, sl_s], lim_ref[:, sl_s]
        s_re, s_im = sre_ref[:, sl_s], sim_ref[:, sl_s]
        res, ims = [], []
        for t in range(t_chunk):
            b_t = bu[n_seq * t:n_seq * (t + 1), :]
            s_re, s_im = (lam_re * s_re - lam_im * s_im + b_t[:, :BLK_STATE],
                          lam_re * s_im + lam_im * s_re + b_t[:, BLK_STATE:])
            res.append(s_re)
            ims.append(s_im)
        sre_ref[:, sl_s] = s_re
        sim_ref[:, sl_s] = s_im
        bu_s[rs, 2 * BLK_STATE * j:2 * BLK_STATE * j + BLK_STATE] = jnp.concatenate(res, axis=0)
        bu_s[rs, 2 * BLK_STATE * j + BLK_STATE:2 * BLK_STATE * (j + 1)] = jnp.concatenate(ims, axis=0)
        return s_re, s_im

    def phase1(c, carry):
        r0 = pl.multiple_of(c * chunk, chunk)
        rs = pl.ds(r0, chunk)
        t0 = pl.multiple_of(c * t_chunk, t_chunk)
        x = x_buf[slot, pl.ds(t0, t_chunk), :, :].reshape(chunk, D_MODEL)
        h = _rmsnorm(x, ng_ref[...]).astype(_BF16)
        h_s[rs, :] = h
        ua = _dot(h, win_ref[:, C_UA:C_ZA])
        ua_s[rs, :] = ua
        ua_bf = ua.astype(_BF16)
        drive = lambda j: _dot(ua_bf[:, LANES * j:LANES * (j + 1)], bmat_ref[j])
        b0, b1 = drive(0), drive(1)
        st = scan_block(0, b0, rs)
        q = _dot(_after(h, *st), win_ref[:, C_Q:C_ZX]) * scale
        for hh in range(N_HEADS):
            q_s[hh, rs, :] = q[:, HEAD_DIM * hh:HEAD_DIM * (hh + 1)]
        b2 = drive(2)
        st = scan_block(1, b1, rs)
        ub_lin = _dot(_after(h, *st), win_ref[:, C_UB:C_UB + E_BR])
        b3 = drive(3)
        st = scan_block(2, b2, rs)
        ub_gate = _dot(_after(h, *st), win_ref[:, C_UB + E_BR:C_ZB])
        full_s[pl.ds(tail_rows + r0, chunk), :] = ub_lin * _sigmoid(ub_gate)
        st = scan_block(3, b3, rs)
        za_s[rs, :] = _silu(_dot(_after(h, *st), win_ref[:, C_ZA:C_UB]))
        return carry

    lax.fori_loop(0, n_chunks, phase1, 0)

    pairs = [(b, hh) for b in range(n_seq) for hh in range(N_HEADS)]
    scores = []
    for b, hh in pairs:
        qb = q_s[hh, pl.ds(b, n_t, stride=n_seq), :].astype(_BF16)
        scores.append(lax.dot_general(qb, k_ref[b, :, HEAD_DIM * hh:HEAD_DIM * (hh + 1)],
                                      (((1,), (1,)), ((), ())), preferred_element_type=_F32))
    probs = [jnp.exp(s - jnp.max(s, axis=-1, keepdims=True)).astype(_BF16) for s in scores]
    outs = [_dot(p, va_ref[b, hh]) for p, (b, hh) in zip(probs, pairs)]
    for o, (b, hh) in zip(outs, pairs):
        o_s[hh, pl.ds(b, n_t, stride=n_seq), :] = o[:, :HEAD_DIM] / o[:, HEAD_DIM:]

    def phase3(c, carry):
        r0 = pl.multiple_of(c * chunk, chunk)
        rs = pl.ds(r0, chunk)
        h = h_s[rs, :]
        ya = _ssm_readout(bu_s[rs, :].astype(_BF16), ua_s[rs, :], cmat_ref, d_ref, wglu_ref, bglu_ref)
        ya = ya * za_s[rs, :]
        m = _sigmoid(_dot(h, win_ref[:, C_GA:C_GB])) * _dot(ya.astype(_BF16), wbra_ref[...])
        acc = _conv_chunk(full_s, cw8_ref, r0, chunk, n_seq) + cb_ref[...]
        cb = _silu(_layernorm(acc, lng_ref[...], lnb_ref[...]))
        cb = cb * _silu(_dot(h, win_ref[:, C_ZB:C_Q]))
        m = m + _sigmoid(_dot(h, win_ref[:, C_GB:C_GX])) * _dot(cb.astype(_BF16), wbrb_ref[...])
        ox = jnp.concatenate([o_s[hh, rs, :] for hh in range(N_HEADS)], axis=-1)
        ox = ox * _silu(_dot(h, win_ref[:, C_ZX:C_GA]))
        m = m + _sigmoid(_dot(h, win_ref[:, C_GX:C_END])) * _dot(ox.astype(_BF16), wbrx_ref[...])
        ts = pl.ds(pl.multiple_of(c * t_chunk, t_chunk), t_chunk)
        out = x_buf[slot, ts, :, :].reshape(chunk, D_MODEL) + _dot(m.astype(_BF16), wout_ref[...])
        if final:
            out = _rmsnorm(out, fg_ref[...])
        y_buf[slot, ts, :, :] = out.reshape(t_chunk, n_seq, D_MODEL)
        return carry

    lax.fori_loop(0, n_chunks, phase3, 0)

    for b in range(n_seq):
        out_copy(i, slot, b).start()

    new_tail = full_s[rows:rows + tail_rows, :]
    full_s[0:tail_rows, :] = new_tail

    @pl.when(i == n_tiles - 1)
    def _fin():
        tail_ref[...] = new_tail
        for b in range(n_seq):
            out_copy(i, slot, b).wait()
        if n_tiles >= 2:
            for b in range(n_seq):
                out_copy(i - 1, 1 - slot, b).wait()


_PROMPT_WEIGHTS = ('norm_g', 'w_in_bf', 'lam_re8', 'lam_im8', 'bmat_bf', 'cmat_bf', 'd', 'w_glu_bf', 'b_glu',
                   'w_br_ssm_bf', 'conv_w8', 'conv_b', 'ln_g', 'ln_b', 'w_br_conv_bf')


def _prompt_layer(x, p, layer, kb, va, final_g, *, rows, chunk, final):
    n_seq, seq, _ = x.shape
    n_t = rows // n_seq
    n_tiles = seq // n_t
    tail_rows = (CONV_K - 1) * n_seq
    kernel = functools.partial(_prompt_kernel, rows=rows, chunk=chunk, n_seq=n_seq, n_tiles=n_tiles, final=final)
    any_spec = pl.BlockSpec(memory_space=pl.ANY)
    stacked = [p[n] for n in _PROMPT_WEIGHTS] + [kb, va, p['w_br_xatt_bf'], p['w_out_bf']]
    state_spec = pl.BlockSpec((n_seq, N_SSM), lambda i: (0, 0))
    return pl.pallas_call(
        kernel,
        grid=(n_tiles,),
        in_specs=[any_spec] + [_layer_spec(a, layer) for a in stacked] + [_const_spec(final_g)],
        out_specs=[any_spec, state_spec, state_spec, pl.BlockSpec((tail_rows, E_BR), lambda i: (0, 0))],
        out_shape=[jax.ShapeDtypeStruct(x.shape, _F32),
                   jax.ShapeDtypeStruct((n_seq, N_SSM), _F32),
                   jax.ShapeDtypeStruct((n_seq, N_SSM), _F32),
                   jax.ShapeDtypeStruct((tail_rows, E_BR), _F32)],
        scratch_shapes=[pltpu.VMEM((2, n_t, n_seq, D_MODEL), _F32),
                        pltpu.VMEM((2, n_t, n_seq, D_MODEL), _F32),
                        pltpu.SemaphoreType.DMA((2, n_seq)),
                        pltpu.SemaphoreType.DMA((2, n_seq)),
                        pltpu.VMEM((rows, D_MODEL), _BF16),
                        pltpu.VMEM((rows, E_BR), _F32),
                        pltpu.VMEM((rows, E_BR), _F32),
                        pltpu.VMEM((rows, 2 * N_SSM), _F32),
                        pltpu.VMEM((tail_rows + rows, E_BR), _F32),
                        pltpu.VMEM((N_HEADS, rows, HEAD_DIM), _F32),
                        pltpu.VMEM((N_HEADS, rows, HEAD_DIM), _F32)],
        compiler_params=pltpu.CompilerParams(dimension_semantics=("arbitrary",), vmem_limit_bytes=VMEM_LIMIT),
        name="prompt_layer",
    )(x, *stacked, final_g)


def _sample_kernel(x_ref, ng_ref, win_ref, lre_ref, lim_ref, bmat_ref, cmat_ref, d_ref,
                   wglu_ref, bglu_ref, wbra_ref, cw_ref, cb_ref, lng_ref, lnb_ref, wbrb_ref,
                   s0re_ref, s0im_ref, cst_ref, k_ref, v_ref, wbrx_ref, wout_ref, fg_ref,
                   y_ref, sre_ref, sim_ref, cnew_ref,
                   x_s, h_s, m_s, q_s, o_s, vb_s, acc_s,
                   *, n, blk):
    layer = pl.program_id(0)
    i = pl.program_id(1)
    scale = 1.0 / math.sqrt(HEAD_DIM)
    groups = KV_ROWS // SUBLANES

    @pl.when((layer == 0) & (i == 0))
    def _load_x():
        x_s[...] = x_ref[...]

    @pl.when(i == 0)
    def _pre():
        h = _rmsnorm(x_s[...], ng_ref[...]).astype(_BF16)
        h_s[...] = h
        ua = _dot(h, win_ref[:, C_UA:C_ZA])
        ua_bf = ua.astype(_BF16)
        xs_parts = []
        for j in range(N_BLK):
            sl_s = slice(BLK_STATE * j, BLK_STATE * (j + 1))
            bu = _dot(ua_bf[:, LANES * j:LANES * (j + 1)], bmat_ref[j])
            l_re, l_im = lre_ref[:, sl_s], lim_ref[:, sl_s]
            o_re, o_im = s0re_ref[:, sl_s], s0im_ref[:, sl_s]
            n_re = l_re * o_re - l_im * o_im + bu[:, :BLK_STATE]
            n_im = l_re * o_im + l_im * o_re + bu[:, BLK_STATE:]
            sre_ref[:, sl_s] = n_re
            sim_ref[:, sl_s] = n_im
            xs_parts += [n_re, n_im]
        xs_bf = jnp.concatenate(xs_parts, axis=-1).astype(_BF16)
        ya = _ssm_readout(xs_bf, ua, cmat_ref, d_ref, wglu_ref, bglu_ref)
        ya = ya * _silu(_dot(h, win_ref[:, C_ZA:C_UB]))
        m_s[...] = _sigmoid(_dot(h, win_ref[:, C_GA:C_GB])) * _dot(ya.astype(_BF16), wbra_ref[...])
        ub = _dot(h, win_ref[:, C_UB:C_ZB])
        vb_s[...] = ub[:, :E_BR] * _sigmoid(ub[:, E_BR:])
        q = _dot(h, win_ref[:, C_Q:C_ZX]) * scale
        for j in range(SUBLANES):
            hh = j % N_HEADS
            q_s[n * j:n * (j + 1), :] = q[:, HEAD_DIM * hh:HEAD_DIM * (hh + 1)]

    ones = jnp.ones((HEAD_DIM, HEAD_DIM), _BF16)

    def fold(t):
        return t, pltpu.roll(t, N_HEADS, axis=0)

    for s in range(blk):
        seq = i * blk + s
        row = pl.ds(seq, 1)
        q8 = q_s[pl.ds(seq, SUBLANES, stride=n), :]
        kq = (k_ref[s].reshape(groups, SUBLANES, HEAD_DIM) * q8[None]).astype(_BF16)
        sc = _dot(kq.reshape(KV_ROWS, HEAD_DIM), ones).reshape(groups, SUBLANES, HEAD_DIM)
        a, b2 = fold(jnp.max(sc, axis=0))
        e = jnp.exp(sc - jnp.maximum(a, b2)[None])
        a, b2 = fold(jnp.sum(e, axis=0))
        den = a + b2
        a, b2 = fold(jnp.sum(e * v_ref[s].reshape(groups, SUBLANES, HEAD_DIM), axis=0))
        o8 = (a + b2) / den
        for hh in range(N_HEADS):
            o_s[hh, row, :] = o8[hh:hh + 1, :]

    rows = pl.ds(pl.multiple_of(i * blk, blk), blk)
    acc = cst_ref[0] * cw_ref[0:1, :]
    for k in range(1, CONV_K - 1):
        acc = acc + cst_ref[k] * cw_ref[k:k + 1, :]
    acc_s[rows, :] = acc
    cnew_ref[0:CONV_K - 2] = cst_ref[1:CONV_K - 1]
    cnew_ref[CONV_K - 2] = vb_s[rows, :]

    @pl.when(i == pl.num_programs(1) - 1)
    def _post():
        h = h_s[...]
        acc = acc_s[...] + cb_ref[...] + cw_ref[CONV_K - 1:CONV_K, :] * vb_s[...]
        cb = _silu(_layernorm(acc, lng_ref[...], lnb_ref[...]))
        cb = cb * _silu(_dot(h, win_ref[:, C_ZB:C_Q]))
        m = m_s[...] + _sigmoid(_dot(h, win_ref[:, C_GB:C_GX])) * _dot(cb.astype(_BF16), wbrb_ref[...])
        ox = jnp.concatenate([o_s[hh] for hh in range(N_HEADS)], axis=-1)
        ox = ox * _silu(_dot(h, win_ref[:, C_ZX:C_GA]))
        m = m + _sigmoid(_dot(h, win_ref[:, C_GX:C_END])) * _dot(ox.astype(_BF16), wbrx_ref[...])
        out = x_s[...] + _dot(m.astype(_BF16), wout_ref[...])
        x_s[...] = out

        @pl.when(layer == pl.num_programs(0) - 1)
        def _final():
            y_ref[...] = _rmsnorm(out, fg_ref[...])


def _sample_trunk(x2d, p, s0re, s0im, cst, kc, vc, final_g, *, blk):
    n = x2d.shape[0]
    depth = cst.shape[0]
    kernel = functools.partial(_sample_kernel, n=n, blk=blk)

    def per_layer(arr):
        nd = arr.ndim - 1
        return pl.BlockSpec((None,) + arr.shape[1:], lambda l, i, _nd=nd: (l,) + (0,) * _nd,
                            pipeline_mode=pl.Buffered(1))

    kv_spec = pl.BlockSpec((None, blk, KV_ROWS, HEAD_DIM), lambda l, i: (l, i, 0, 0))
    win_spec = pl.BlockSpec((None, CONV_K - 1, blk, E_BR), lambda l, i: (l, 0, i, 0))
    mid = [p[nm] for nm in ('lam_re1', 'lam_im1', 'bmat_bf', 'cmat_bf', 'd', 'w_glu_bf', 'b_glu', 'w_br_ssm_bf',
                            'conv_w', 'conv_b', 'ln_g', 'ln_b', 'w_br_conv_bf')]
    ins = [x2d, p['norm_g'], p['w_in_bf']] + mid + [s0re, s0im, cst, kc, vc, p['w_br_xatt_bf'], p['w_out_bf'],
                                                    final_g]
    in_specs = ([_const_spec(x2d), per_layer(p['norm_g']), per_layer(p['w_in_bf'])] + [per_layer(a) for a in mid]
                + [per_layer(s0re), per_layer(s0im), win_spec, kv_spec, kv_spec, per_layer(p['w_br_xatt_bf']),
                   per_layer(p['w_out_bf']), _const_spec(final_g)])
    state_spec = pl.BlockSpec((None, n, N_SSM), lambda l, i: (l, 0, 0))
    return pl.pallas_call(
        kernel,
        grid=(depth, n // blk),
        in_specs=in_specs,
        out_specs=[pl.BlockSpec((n, D_MODEL), lambda l, i: (0, 0)), state_spec, state_spec, win_spec],
        out_shape=[jax.ShapeDtypeStruct((n, D_MODEL), _F32),
                   jax.ShapeDtypeStruct((depth, n, N_SSM), _F32),
                   jax.ShapeDtypeStruct((depth, n, N_SSM), _F32),
                   jax.ShapeDtypeStruct(cst.shape, _F32)],
        scratch_shapes=[pltpu.VMEM((n, D_MODEL), _F32),
                        pltpu.VMEM((n, D_MODEL), _BF16),
                        pltpu.VMEM((n, D_MODEL), _F32),
                        pltpu.VMEM((SUBLANES * n, HEAD_DIM), _F32),
                        pltpu.VMEM((N_HEADS, n, HEAD_DIM), _F32),
                        pltpu.VMEM((n, E_BR), _F32),
                        pltpu.VMEM((n, E_BR), _F32)],
        compiler_params=pltpu.CompilerParams(dimension_semantics=("arbitrary", "arbitrary"),
                                             vmem_limit_bytes=VMEM_LIMIT),
        name="sample_trunk",
    )(*ins)


def _prep(norm_g, w_in, a_re, a_im, log_dt, b_re, b_im, c_re, c_im, d, w_glu, b_glu, w_br_ssm,
          conv_w, conv_b, ln_g, ln_b, w_br_conv, w_br_xatt, w_out):
    depth = norm_g.shape[0]
    dt = jnp.exp(log_dt)[..., None]
    mag = jnp.exp(a_re * dt)
    lr = mag * jnp.cos(a_im * dt)
    li = mag * jnp.sin(a_im * dt)
    den = a_re * a_re + a_im * a_im
    cr = ((lr - 1.0) * a_re + li * a_im) / den
    ci = (li * a_re - (lr - 1.0) * a_im) / den
    bb_re = cr[..., None] * b_re - ci[..., None] * b_im
    bb_im = cr[..., None] * b_im + ci[..., None] * b_re
    gpb = N_GROUPS // N_BLK
    eye = jnp.eye(gpb, dtype=_F32)

    def b_blocks(bb):
        bb = bb.reshape(depth, N_BLK, gpb, N_STATE, GROUP)
        return jnp.einsum('ljgph,gk->ljghkp', bb, eye).reshape(depth, N_BLK, LANES, BLK_STATE)

    def c_blocks(cc):
        cc = cc.reshape(depth, N_BLK, gpb, GROUP, N_STATE)
        return jnp.einsum('ljghp,gk->ljkpgh', cc, eye).reshape(depth, N_BLK, BLK_STATE, LANES)

    bmat = jnp.concatenate([b_blocks(bb_re), b_blocks(bb_im)], axis=-1)
    cmat = jnp.concatenate([c_blocks(c_re), -c_blocks(c_im)], axis=2)
    lr1 = lr.reshape(depth, 1, N_SSM)
    li1 = li.reshape(depth, 1, N_SSM)
    row = lambda v: v.reshape(depth, 1, -1)
    return {
        'norm_g': row(norm_g), 'w_in_bf': w_in.astype(_BF16),
        'lam_re1': lr1, 'lam_im1': li1,
        'lam_re8': jnp.broadcast_to(lr1, (depth, SUBLANES, N_SSM)),
        'lam_im8': jnp.broadcast_to(li1, (depth, SUBLANES, N_SSM)),
        'bmat_bf': bmat.astype(_BF16), 'cmat_bf': cmat.astype(_BF16),
        'd': row(d), 'w_glu_bf': w_glu.astype(_BF16), 'b_glu': row(b_glu),
        'w_br_ssm_bf': w_br_ssm.astype(_BF16), 'conv_w': conv_w, 'conv_w8': jnp.repeat(conv_w, SUBLANES, axis=1),
        'conv_b': row(conv_b),
        'ln_g': row(ln_g), 'ln_b': row(ln_b), 'w_br_conv_bf': w_br_conv.astype(_BF16),
        'w_br_xatt_bf': w_br_xatt.astype(_BF16), 'w_out_bf': w_out.astype(_BF16),
    }


def kernel(x_prompt, x_sample, mem_prompt, state_ssm_re, state_ssm_im, state_conv, cache_mem_k, cache_mem_v,
           norm_g, w_in, ssm_a_re, ssm_a_im, ssm_log_dt, ssm_b_re, ssm_b_im, ssm_c_re, ssm_c_im, ssm_d,
           w_glu, b_glu, w_br_ssm, conv_w, conv_b, conv_ln_g, conv_ln_b, w_br_conv,
           mem_norm_g, w_k, w_v, w_br_xatt, w_out, final_norm_g):
    bp, seq, _ = x_prompt.shape
    bs = x_sample.shape[0]
    depth = norm_g.shape[0]
    assert bp == SUBLANES and x_sample.shape[1] == 1 and depth == DEPTH
    rows, chunk, sample_blk = PROMPT_TILE_ROWS, PROMPT_CHUNK_ROWS, SAMPLE_BLOCK

    p = _prep(norm_g, w_in, ssm_a_re, ssm_a_im, ssm_log_dt, ssm_b_re, ssm_b_im, ssm_c_re, ssm_c_im, ssm_d,
              w_glu, b_glu, w_br_ssm, conv_w, conv_b, conv_ln_g, conv_ln_b, w_br_conv, w_br_xatt, w_out)
    final_g = final_norm_g.reshape(1, D_MODEL)
    wkv = jnp.concatenate([w_k, w_v], axis=-1).astype(_BF16)
    mem_k, mem_v, kb, va = _memkv(mem_prompt, mem_norm_g.reshape(depth, 1, D_MODEL), wkv)

    xp = x_prompt
    xs = x_sample.reshape(bs, D_MODEL)
    s0re = state_ssm_re.reshape(depth, bs, N_SSM)
    s0im = state_ssm_im.reshape(depth, bs, N_SSM)
    kc = cache_mem_k.reshape(depth, bs, KV_ROWS, HEAD_DIM)
    vc = cache_mem_v.reshape(depth, bs, KV_ROWS, HEAD_DIM)

    p_re, p_im, p_conv = [], [], []
    for l in range(depth):
        final = l == depth - 1
        xp, re, im, tail = _prompt_layer(xp, p, l, kb, va, final_g, rows=rows, chunk=chunk, final=final)
        p_re.append(re.reshape(bp, N_GROUPS, N_STATE))
        p_im.append(im.reshape(bp, N_GROUPS, N_STATE))
        p_conv.append(tail.reshape(CONV_K - 1, bp, E_BR).transpose(1, 0, 2))

    ys, s_re, s_im, s_conv = _sample_trunk(xs, p, s0re, s0im, state_conv.transpose(0, 2, 1, 3), kc, vc, final_g,
                                           blk=sample_blk)
    s_conv = s_conv.transpose(0, 2, 1, 3)

    y_prompt = xp
    y_sample = ys.reshape(bs, 1, D_MODEL)
    ssm_shape = (depth, bs, N_GROUPS, N_STATE)
    kv_shape = (depth, bp, N_MEM, N_HEADS, HEAD_DIM)
    return (y_prompt, y_sample, jnp.stack(p_re), jnp.stack(p_im), jnp.stack(p_conv), mem_k.reshape(kv_shape),
            mem_v.reshape(kv_shape), s_re.reshape(ssm_shape), s_im.reshape(ssm_shape), s_conv)
```

```python
import functools
import math

import jax
import jax.numpy as jnp
from jax import lax
from jax.experimental import pallas as pl
from jax.experimental.pallas import tpu as pltpu

D_MODEL = 1024
E_BR = 512
N_GROUPS = 32
GROUP = 16
N_STATE = 64
N_SSM = N_GROUPS * N_STATE
CONV_K = 31
N_HEADS = 4
HEAD_DIM = 128
N_MEM = 256
EPS = 1e-6
DEPTH = 2

LANES = 128
SUBLANES = 8
N_BLK = E_BR // LANES
BLK_STATE = N_SSM // N_BLK
KV_ROWS = N_MEM * N_HEADS

C_UA, C_ZA, C_UB, C_ZB, C_Q, C_ZX, C_GA, C_GB, C_GX, C_END = (
    0, 512, 1024, 2048, 2560, 3072, 3584, 4608, 5632, 6656)

VMEM_LIMIT = 60000 * 1024

PROMPT_TILE_ROWS = 512
PROMPT_CHUNK_ROWS = 256
CONV_PIECE_ROWS = 64
SAMPLE_BLOCK = 8
MEMKV_BLOCK = 8

_F32 = jnp.float32
_BF16 = jnp.bfloat16


def _dot(a, b):
    return jnp.dot(a, b, preferred_element_type=_F32)


def _sigmoid(x):
    return jax.nn.sigmoid(x)


def _silu(x):
    return x * _sigmoid(x)


def _rmsnorm(x, g):
    return x * lax.rsqrt(jnp.mean(x * x, axis=-1, keepdims=True) + EPS) * g


def _layernorm(x, g, b):
    mu = jnp.mean(x, axis=-1, keepdims=True)
    xc = x - mu
    var = jnp.mean(xc * xc, axis=-1, keepdims=True)
    return xc * lax.rsqrt(var + EPS) * g + b


def _layer_spec(arr, layer):
    nd = arr.ndim - 1
    return pl.BlockSpec((None,) + arr.shape[1:], lambda *_, _l=layer, _nd=nd: (_l,) + (0,) * _nd,
                        pipeline_mode=pl.Buffered(1))


def _const_spec(arr):
    nd = arr.ndim
    return pl.BlockSpec(arr.shape, lambda *_, _nd=nd: (0,) * _nd, pipeline_mode=pl.Buffered(1))


def _memkv_kernel(mem_ref, g_ref, wkv_ref, k_ref, v_ref, kb_ref, va_ref, *, n_b):
    hm = _rmsnorm(mem_ref[...].reshape(n_b * N_MEM, D_MODEL), g_ref[...]).astype(_BF16)
    kv = _dot(hm, wkv_ref[...])
    ones = jnp.ones((N_MEM, HEAD_DIM), _BF16)
    for b in range(n_b):
        kvb = kv[N_MEM * b:N_MEM * (b + 1), :]
        kb_ref[b] = kvb[:, :E_BR].astype(_BF16)
        for hh in range(N_HEADS):
            kh = kvb[:, HEAD_DIM * hh:HEAD_DIM * (hh + 1)]
            vh = kvb[:, E_BR + HEAD_DIM * hh:E_BR + HEAD_DIM * (hh + 1)]
            k_ref[b, pl.ds(hh, N_MEM, stride=N_HEADS), :] = kh
            v_ref[b, pl.ds(hh, N_MEM, stride=N_HEADS), :] = vh
            va_ref[b, hh, :, 0:HEAD_DIM] = vh.astype(_BF16)
            va_ref[b, hh, :, HEAD_DIM:2 * HEAD_DIM] = ones


def _memkv(mem, g, wkv, n_b=MEMKV_BLOCK):
    depth, bp = g.shape[0], mem.shape[0]
    kv_f = jax.ShapeDtypeStruct((depth, bp, KV_ROWS, HEAD_DIM), _F32)
    kv_spec = pl.BlockSpec((None, n_b, KV_ROWS, HEAD_DIM), lambda l, b: (l, b, 0, 0))
    return pl.pallas_call(
        functools.partial(_memkv_kernel, n_b=n_b),
        grid=(depth, bp // n_b),
        in_specs=[pl.BlockSpec((n_b, N_MEM, D_MODEL), lambda l, b: (b, 0, 0)),
                  pl.BlockSpec((None, 1, D_MODEL), lambda l, b: (l, 0, 0)),
                  pl.BlockSpec((None, D_MODEL, 2 * E_BR), lambda l, b: (l, 0, 0))],
        out_specs=[kv_spec, kv_spec,
                   pl.BlockSpec((None, n_b, N_MEM, E_BR), lambda l, b: (l, b, 0, 0)),
                   pl.BlockSpec((None, n_b, N_HEADS, N_MEM, 2 * HEAD_DIM), lambda l, b: (l, b, 0, 0, 0))],
        out_shape=[kv_f, kv_f,
                   jax.ShapeDtypeStruct((depth, bp, N_MEM, E_BR), _BF16),
                   jax.ShapeDtypeStruct((depth, bp, N_HEADS, N_MEM, 2 * HEAD_DIM), _BF16)],
        compiler_params=pltpu.CompilerParams(dimension_semantics=("arbitrary", "arbitrary"),
                                             vmem_limit_bytes=VMEM_LIMIT),
        name="mem_kv",
    )(mem, g, wkv)


def _ssm_readout(xs_bf, ua, cmat_ref, d_ref, wglu_ref, bglu_ref):
    ys = [_dot(xs_bf[:, 2 * BLK_STATE * j:2 * BLK_STATE * (j + 1)], cmat_ref[j]) for j in range(N_BLK)]
    y = jnp.concatenate(ys, axis=-1) + d_ref[...] * ua
    y = jax.nn.gelu(y, approximate=True)
    return y * _sigmoid(_dot(y.astype(_BF16), wglu_ref[...]) + bglu_ref[...])


def _zero_of(*parts):
    acc = None
    for v in parts:
        for r in range(0, v.shape[0], SUBLANES):
            for c in range(0, v.shape[1], LANES):
                t = v[r:r + SUBLANES, c:c + LANES]
                acc = t if acc is None else acc + t
    return acc * 0.0


def _after(lhs, *parts):
    acc = _zero_of(*parts)
    zero = jnp.concatenate([acc, acc], axis=0).astype(_BF16)
    top = jnp.concatenate([lhs[0:2 * SUBLANES, 0:LANES] + zero, lhs[0:2 * SUBLANES, LANES:]], axis=1)
    return jnp.concatenate([top, lhs[2 * SUBLANES:]], axis=0)


def _conv_chunk(full_s, cw8_ref, r0, chunk, n_seq, piece=CONV_PIECE_ROWS):
    out = []
    for p in range(chunk // piece):
        acc = None
        for k in range(CONV_K):
            w = cw8_ref[SUBLANES * k:SUBLANES * (k + 1), :]
            win = full_s[pl.ds(r0 + p * piece + k * n_seq, piece), :].reshape(piece // SUBLANES, SUBLANES, E_BR)
            term = win * w[None]
            acc = term if acc is None else acc + term
        out.append(acc.reshape(piece, E_BR))
    return jnp.concatenate(out, axis=0)


def _prompt_kernel(x_hbm, ng_ref, win_ref, lre_ref, lim_ref, bmat_ref, cmat_ref, d_ref, wglu_ref, bglu_ref,
                   wbra_ref, cw8_ref, cb_ref, lng_ref, lnb_ref, wbrb_ref, k_ref, va_ref, wbrx_ref, wout_ref,
                   fg_ref,
                   y_hbm, sre_ref, sim_ref, tail_ref,
                   x_buf, y_buf, sem_in, sem_out, h_s, ua_s, za_s, conv_s, bu_s, full_s, q_s, o_s,
                   *, rows, chunk, n_seq, n_tiles, final):
    i = pl.program_id(0)
    n_t = rows // n_seq
    tail_rows = (CONV_K - 1) * n_seq
    n_chunks = rows // chunk
    t_chunk = chunk // n_seq
    scale = 1.0 / math.sqrt(HEAD_DIM)
    slot = lax.rem(i, 2)

    def in_copy(tile, slt, b):
        return pltpu.make_async_copy(x_hbm.at[b, pl.ds(tile * n_t, n_t), :], x_buf.at[slt, :, b, :],
                                     sem_in.at[slt, b])

    def out_copy(tile, slt, b):
        return pltpu.make_async_copy(y_buf.at[slt, :, b, :], y_hbm.at[b, pl.ds(tile * n_t, n_t), :],
                                     sem_out.at[slt, b])

    @pl.when(i == 0)
    def _first_fetch():
        for b in range(n_seq):
            in_copy(0, 0, b).start()

    @pl.when(i + 1 < n_tiles)
    def _prefetch():
        for b in range(n_seq):
            in_copy(i + 1, 1 - slot, b).start()

    for b in range(n_seq):
        in_copy(i, slot, b).wait()

    @pl.when(i >= 2)
    def _drain_old_output():
        for b in range(n_seq):
            out_copy(i - 2, slot, b).wait()

    @pl.when(i == 0)
    def _init():
        sre_ref[...] = jnp.zeros_like(sre_ref)
        sim_ref[...] = jnp.zeros_like(sim_ref)
        full_s[0:tail_rows, :] = jnp.zeros((tail_rows, E_BR), _F32)

    def scan_block(j, bu, rs):
        sl_s = slice(BLK_STATE * j, BLK_STATE * (j + 1))
        lam_re, lam_im = lre_ref[:, sl_s], lim_ref[:, sl_s]
        s_re, s_im = sre_ref[:, sl_s], sim_ref[:, sl_s]
        res, ims = [], []
        for t in range(t_chunk):
            b_t = bu[n_seq * t:n_seq * (t + 1), :]
            s_re, s_im = (lam_re * s_re - lam_im * s_im + b_t[:, :BLK_STATE],
                          lam_re * s_im + lam_im * s_re + b_t[:, BLK_STATE:])
            res.append(s_re)
            ims.append(s_im)
        sre_ref[:, sl_s] = s_re
        sim_ref[:, sl_s] = s_im
        bu_s[rs, 2 * BLK_STATE * j:2 * BLK_STATE * j + BLK_STATE] = jnp.concatenate(res, axis=0)
        bu_s[rs, 2 * BLK_STATE * j + BLK_STATE:2 * BLK_STATE * (j + 1)] = jnp.concatenate(ims, axis=0)
        return s_re, s_im

    def phase1(c, prev_conv):
        r0 = c * chunk
        rs = slice(r0, r0 + chunk)
        x = x_buf[slot, c * t_chunk:(c + 1) * t_chunk, :, :].reshape(chunk, D_MODEL)
        h = _rmsnorm(x, ng_ref[...]).astype(_BF16)
        h_s[rs, :] = h
        pending = list(prev_conv)

        def ordered(*state):
            return state + ((pending.pop(0),) if pending else ())

        ua = _dot(_after(h, *ordered()) if pending else h, win_ref[:, C_UA:C_ZA])
        ua_s[rs, :] = ua
        ua_bf = ua.astype(_BF16)
        drive = lambda j: _dot(ua_bf[:, LANES * j:LANES * (j + 1)], bmat_ref[j])
        b0, b1 = drive(0), drive(1)
        st = scan_block(0, b0, rs)
        q = _dot(_after(h, *ordered(*st)), win_ref[:, C_Q:C_ZX]) * scale
        for hh in range(N_HEADS):
            q_s[hh, rs, :] = q[:, HEAD_DIM * hh:HEAD_DIM * (hh + 1)]
        b2 = drive(2)
        st = scan_block(1, b1, rs)
        ub_lin = _dot(_after(h, *ordered(*st)), win_ref[:, C_UB:C_UB + E_BR])
        b3 = drive(3)
        st = scan_block(2, b2, rs)
        ub_gate = _dot(_after(h, *ordered(*st)), win_ref[:, C_UB + E_BR:C_ZB])
        full_s[tail_rows + r0:tail_rows + r0 + chunk, :] = ub_lin * _sigmoid(ub_gate)
        st = scan_block(3, b3, rs)
        za_s[rs, :] = _silu(_dot(_after(h, *st), win_ref[:, C_ZA:C_UB]))

    def conv_pieces(c):
        pieces = [_conv_chunk(full_s, cw8_ref, c * chunk + p * CONV_PIECE_ROWS, CONV_PIECE_ROWS, n_seq)
                  for p in range(chunk // CONV_PIECE_ROWS)]
        conv_s[c * chunk:(c + 1) * chunk, :] = jnp.concatenate(pieces, axis=0)
        return pieces

    prev = []
    for c in range(n_chunks):
        phase1(c, prev)
        prev = conv_pieces(c) if c + 1 < n_chunks else []

    pairs = [(b, hh) for b in range(n_seq) for hh in range(N_HEADS)]
    scores = []
    for b, hh in pairs:
        qb = q_s[hh, pl.ds(b, n_t, stride=n_seq), :].astype(_BF16)
        scores.append(lax.dot_general(qb, k_ref[b, :, HEAD_DIM * hh:HEAD_DIM * (hh + 1)],
                                      (((1,), (1,)), ((), ())), preferred_element_type=_F32))
    probs = [jnp.exp(s - jnp.max(s, axis=-1, keepdims=True)).astype(_BF16) for s in scores]
    outs = [_dot(p, va_ref[b, hh]) for p, (b, hh) in zip(probs, pairs)]
    for o, (b, hh) in zip(outs, pairs):
        o_s[hh, pl.ds(b, n_t, stride=n_seq), :] = o[:, :HEAD_DIM] / o[:, HEAD_DIM:]

    def phase3(c):
        r0 = c * chunk
        rs = slice(r0, r0 + chunk)
        h = h_s[rs, :]
        ya = _ssm_readout(bu_s[rs, :].astype(_BF16), ua_s[rs, :], cmat_ref, d_ref, wglu_ref, bglu_ref)
        ya = ya * za_s[rs, :]
        m = _sigmoid(_dot(h, win_ref[:, C_GA:C_GB])) * _dot(ya.astype(_BF16), wbra_ref[...])
        conv = conv_s[rs, :] if c + 1 < n_chunks else _conv_chunk(full_s, cw8_ref, r0, chunk, n_seq)
        acc = conv + cb_ref[...]
        cb = _silu(_layernorm(acc, lng_ref[...], lnb_ref[...]))
        cb = cb * _silu(_dot(h, win_ref[:, C_ZB:C_Q]))
        m = m + _sigmoid(_dot(h, win_ref[:, C_GB:C_GX])) * _dot(cb.astype(_BF16), wbrb_ref[...])
        ox = jnp.concatenate([o_s[hh, rs, :] for hh in range(N_HEADS)], axis=-1)
        ox = ox * _silu(_dot(h, win_ref[:, C_ZX:C_GA]))
        m = m + _sigmoid(_dot(h, win_ref[:, C_GX:C_END])) * _dot(ox.astype(_BF16), wbrx_ref[...])
        ts = slice(c * t_chunk, (c + 1) * t_chunk)
        out = x_buf[slot, ts, :, :].reshape(chunk, D_MODEL) + _dot(m.astype(_BF16), wout_ref[...])
        if final:
            out = _rmsnorm(out, fg_ref[...])
        y_buf[slot, ts, :, :] = out.reshape(t_chunk, n_seq, D_MODEL)

    for c in range(n_chunks):
        phase3(c)

    for b in range(n_seq):
        out_copy(i, slot, b).start()

    new_tail = full_s[rows:rows + tail_rows, :]
    full_s[0:tail_rows, :] = new_tail

    @pl.when(i == n_tiles - 1)
    def _fin():
        tail_ref[...] = new_tail
        for b in range(n_seq):
            out_copy(i, slot, b).wait()
        if n_tiles >= 2:
            for b in range(n_seq):
                out_copy(i - 1, 1 - slot, b).wait()


_PROMPT_WEIGHTS = ('norm_g', 'w_in_bf', 'lam_re8', 'lam_im8', 'bmat_bf', 'cmat_bf', 'd', 'w_glu_bf', 'b_glu',
                   'w_br_ssm_bf', 'conv_w8', 'conv_b', 'ln_g', 'ln_b', 'w_br_conv_bf')


def _prompt_layer(x, p, layer, kb, va, final_g, *, rows, chunk, final):
    n_seq, seq, _ = x.shape
    n_t = rows // n_seq
    n_tiles = seq // n_t
    tail_rows = (CONV_K - 1) * n_seq
    kernel = functools.partial(_prompt_kernel, rows=rows, chunk=chunk, n_seq=n_seq, n_tiles=n_tiles, final=final)
    any_spec = pl.BlockSpec(memory_space=pl.ANY)
    stacked = [p[n] for n in _PROMPT_WEIGHTS] + [kb, va, p['w_br_xatt_bf'], p['w_out_bf']]
    state_spec = pl.BlockSpec((n_seq, N_SSM), lambda i: (0, 0))
    return pl.pallas_call(
        kernel,
        grid=(n_tiles,),
        in_specs=[any_spec] + [_layer_spec(a, layer) for a in stacked] + [_const_spec(final_g)],
        out_specs=[any_spec, state_spec, state_spec, pl.BlockSpec((tail_rows, E_BR), lambda i: (0, 0))],
        out_shape=[jax.ShapeDtypeStruct(x.shape, _F32),
                   jax.ShapeDtypeStruct((n_seq, N_SSM), _F32),
                   jax.ShapeDtypeStruct((n_seq, N_SSM), _F32),
                   jax.ShapeDtypeStruct((tail_rows, E_BR), _F32)],
        scratch_shapes=[pltpu.VMEM((2, n_t, n_seq, D_MODEL), _F32),
                        pltpu.VMEM((2, n_t, n_seq, D_MODEL), _F32),
                        pltpu.SemaphoreType.DMA((2, n_seq)),
                        pltpu.SemaphoreType.DMA((2, n_seq)),
                        pltpu.VMEM((rows, D_MODEL), _BF16),
                        pltpu.VMEM((rows, E_BR), _F32),
                        pltpu.VMEM((rows, E_BR), _F32),
                        pltpu.VMEM((rows, E_BR), _F32),
                        pltpu.VMEM((rows, 2 * N_SSM), _F32),
                        pltpu.VMEM((tail_rows + rows, E_BR), _F32),
                        pltpu.VMEM((N_HEADS, rows, HEAD_DIM), _F32),
                        pltpu.VMEM((N_HEADS, rows, HEAD_DIM), _F32)],
        compiler_params=pltpu.CompilerParams(dimension_semantics=("arbitrary",), vmem_limit_bytes=VMEM_LIMIT),
        name="prompt_layer",
    )(x, *stacked, final_g)


def _sample_kernel(x_ref, ng_ref, win_ref, lre_ref, lim_ref, bmat_ref, cmat_ref, d_ref,
                   wglu_ref, bglu_ref, wbra_ref, cw_ref, cb_ref, lng_ref, lnb_ref, wbrb_ref,
                   s0re_ref, s0im_ref, cst_ref, k_ref, v_ref, wbrx_ref, wout_ref, fg_ref,
                   y_ref, sre_ref, sim_ref, cnew_ref,
                   x_s, h_s, m_s, q_s, o_s, vb_s, acc_s,
                   *, n, blk):
    layer = pl.program_id(0)
    i = pl.program_id(1)
    scale = 1.0 / math.sqrt(HEAD_DIM)
    groups = KV_ROWS // SUBLANES

    @pl.when((layer == 0) & (i == 0))
    def _load_x():
        x_s[...] = x_ref[...]

    @pl.when(i == 0)
    def _pre():
        h = _rmsnorm(x_s[...], ng_ref[...]).astype(_BF16)
        h_s[...] = h
        ua = _dot(h, win_ref[:, C_UA:C_ZA])
        ua_bf = ua.astype(_BF16)
        xs_parts = []
        for j in range(N_BLK):
            sl_s = slice(BLK_STATE * j, BLK_STATE * (j + 1))
            bu = _dot(ua_bf[:, LANES * j:LANES * (j + 1)], bmat_ref[j])
            l_re, l_im = lre_ref[:, sl_s], lim_ref[:, sl_s]
            o_re, o_im = s0re_ref[:, sl_s], s0im_ref[:, sl_s]
            n_re = l_re * o_re - l_im * o_im + bu[:, :BLK_STATE]
            n_im = l_re * o_im + l_im * o_re + bu[:, BLK_STATE:]
            sre_ref[:, sl_s] = n_re
            sim_ref[:, sl_s] = n_im
            xs_parts += [n_re, n_im]
        xs_bf = jnp.concatenate(xs_parts, axis=-1).astype(_BF16)
        ya = _ssm_readout(xs_bf, ua, cmat_ref, d_ref, wglu_ref, bglu_ref)
        ya = ya * _silu(_dot(h, win_ref[:, C_ZA:C_UB]))
        m_s[...] = _sigmoid(_dot(h, win_ref[:, C_GA:C_GB])) * _dot(ya.astype(_BF16), wbra_ref[...])
        ub = _dot(h, win_ref[:, C_UB:C_ZB])
        vb_s[...] = ub[:, :E_BR] * _sigmoid(ub[:, E_BR:])
        q = _dot(h, win_ref[:, C_Q:C_ZX]) * scale
        for j in range(SUBLANES):
            hh = j % N_HEADS
            q_s[n * j:n * (j + 1), :] = q[:, HEAD_DIM * hh:HEAD_DIM * (hh + 1)]

    ones = jnp.ones((HEAD_DIM, HEAD_DIM), _BF16)

    def fold(t):
        return t, pltpu.roll(t, N_HEADS, axis=0)

    for s in range(blk):
        seq = i * blk + s
        row = pl.ds(seq, 1)
        q8 = q_s[pl.ds(seq, SUBLANES, stride=n), :]
        kq = (k_ref[s].reshape(groups, SUBLANES, HEAD_DIM) * q8[None]).astype(_BF16)
        sc = _dot(kq.reshape(KV_ROWS, HEAD_DIM), ones).reshape(groups, SUBLANES, HEAD_DIM)
        a, b2 = fold(jnp.max(sc, axis=0))
        e = jnp.exp(sc - jnp.maximum(a, b2)[None])
        a, b2 = fold(jnp.sum(e, axis=0))
        den = a + b2
        a, b2 = fold(jnp.sum(e * v_ref[s].reshape(groups, SUBLANES, HEAD_DIM), axis=0))
        o8 = (a + b2) / den
        for hh in range(N_HEADS):
            o_s[hh, row, :] = o8[hh:hh + 1, :]

    rows = pl.ds(pl.multiple_of(i * blk, blk), blk)
    acc = cst_ref[0] * cw_ref[0:1, :]
    for k in range(1, CONV_K - 1):
        acc = acc + cst_ref[k] * cw_ref[k:k + 1, :]
    acc_s[rows, :] = acc
    cnew_ref[0:CONV_K - 2] = cst_ref[1:CONV_K - 1]
    cnew_ref[CONV_K - 2] = vb_s[rows, :]

    @pl.when(i == pl.num_programs(1) - 1)
    def _post():
        h = h_s[...]
        acc = acc_s[...] + cb_ref[...] + cw_ref[CONV_K - 1:CONV_K, :] * vb_s[...]
        cb = _silu(_layernorm(acc, lng_ref[...], lnb_ref[...]))
        cb = cb * _silu(_dot(h, win_ref[:, C_ZB:C_Q]))
        m = m_s[...] + _sigmoid(_dot(h, win_ref[:, C_GB:C_GX])) * _dot(cb.astype(_BF16), wbrb_ref[...])
        ox = jnp.concatenate([o_s[hh] for hh in range(N_HEADS)], axis=-1)
        ox = ox * _silu(_dot(h, win_ref[:, C_ZX:C_GA]))
        m = m + _sigmoid(_dot(h, win_ref[:, C_GX:C_END])) * _dot(ox.astype(_BF16), wbrx_ref[...])
        out = x_s[...] + _dot(m.astype(_BF16), wout_ref[...])
        x_s[...] = out

        @pl.when(layer == pl.num_programs(0) - 1)
        def _final():
            y_ref[...] = _rmsnorm(out, fg_ref[...])


def _sample_trunk(x2d, p, s0re, s0im, cst, kc, vc, final_g, *, blk):
    n = x2d.shape[0]
    depth = cst.shape[0]
    kernel = functools.partial(_sample_kernel, n=n, blk=blk)

    def per_layer(arr):
        nd = arr.ndim - 1
        return pl.BlockSpec((None,) + arr.shape[1:], lambda l, i, _nd=nd: (l,) + (0,) * _nd,
                            pipeline_mode=pl.Buffered(1))

    kv_spec = pl.BlockSpec((None, blk, KV_ROWS, HEAD_DIM), lambda l, i: (l, i, 0, 0))
    win_spec = pl.BlockSpec((None, CONV_K - 1, blk, E_BR), lambda l, i: (l, 0, i, 0))
    mid = [p[nm] for nm in ('lam_re1', 'lam_im1', 'bmat_bf', 'cmat_bf', 'd', 'w_glu_bf', 'b_glu', 'w_br_ssm_bf',
                            'conv_w', 'conv_b', 'ln_g', 'ln_b', 'w_br_conv_bf')]
    ins = [x2d, p['norm_g'], p['w_in_bf']] + mid + [s0re, s0im, cst, kc, vc, p['w_br_xatt_bf'], p['w_out_bf'],
                                                    final_g]
    in_specs = ([_const_spec(x2d), per_layer(p['norm_g']), per_layer(p['w_in_bf'])] + [per_layer(a) for a in mid]
                + [per_layer(s0re), per_layer(s0im), win_spec, kv_spec, kv_spec, per_layer(p['w_br_xatt_bf']),
                   per_layer(p['w_out_bf']), _const_spec(final_g)])
    state_spec = pl.BlockSpec((None, n, N_SSM), lambda l, i: (l, 0, 0))
    return pl.pallas_call(
        kernel,
        grid=(depth, n // blk),
        in_specs=in_specs,
        out_specs=[pl.BlockSpec((n, D_MODEL), lambda l, i: (0, 0)), state_spec, state_spec, win_spec],
        out_shape=[jax.ShapeDtypeStruct((n, D_MODEL), _F32),
                   jax.ShapeDtypeStruct((depth, n, N_SSM), _F32),
                   jax.ShapeDtypeStruct((depth, n, N_SSM), _F32),
                   jax.ShapeDtypeStruct(cst.shape, _F32)],
        scratch_shapes=[pltpu.VMEM((n, D_MODEL), _F32),
                        pltpu.VMEM((n, D_MODEL), _BF16),
                        pltpu.VMEM((n, D_MODEL), _F32),
                        pltpu.VMEM((SUBLANES * n, HEAD_DIM), _F32),
                        pltpu.VMEM((N_HEADS, n, HEAD_DIM), _F32),
                        pltpu.VMEM((n, E_BR), _F32),
                        pltpu.VMEM((n, E_BR), _F32)],
        compiler_params=pltpu.CompilerParams(dimension_semantics=("arbitrary", "arbitrary"),
                                             vmem_limit_bytes=VMEM_LIMIT),
        name="sample_trunk",
    )(*ins)


def _prep(norm_g, w_in, a_re, a_im, log_dt, b_re, b_im, c_re, c_im, d, w_glu, b_glu, w_br_ssm,
          conv_w, conv_b, ln_g, ln_b, w_br_conv, w_br_xatt, w_out):
    depth = norm_g.shape[0]
    dt = jnp.exp(log_dt)[..., None]
    mag = jnp.exp(a_re * dt)
    lr = mag * jnp.cos(a_im * dt)
    li = mag * jnp.sin(a_im * dt)
    den = a_re * a_re + a_im * a_im
    cr = ((lr - 1.0) * a_re + li * a_im) / den
    ci = (li * a_re - (lr - 1.0) * a_im) / den
    bb_re = cr[..., None] * b_re - ci[..., None] * b_im
    bb_im = cr[..., None] * b_im + ci[..., None] * b_re
    gpb = N_GROUPS // N_BLK
    eye = jnp.eye(gpb, dtype=_F32)

    def b_blocks(bb):
        bb = bb.reshape(depth, N_BLK, gpb, N_STATE, GROUP)
        return jnp.einsum('ljgph,gk->ljghkp', bb, eye).reshape(depth, N_BLK, LANES, BLK_STATE)

    def c_blocks(cc):
        cc = cc.reshape(depth, N_BLK, gpb, GROUP, N_STATE)
        return jnp.einsum('ljghp,gk->ljkpgh', cc, eye).reshape(depth, N_BLK, BLK_STATE, LANES)

    bmat = jnp.concatenate([b_blocks(bb_re), b_blocks(bb_im)], axis=-1)
    cmat = jnp.concatenate([c_blocks(c_re), -c_blocks(c_im)], axis=2)
    lr1 = lr.reshape(depth, 1, N_SSM)
    li1 = li.reshape(depth, 1, N_SSM)
    row = lambda v: v.reshape(depth, 1, -1)
    return {
        'norm_g': row(norm_g), 'w_in_bf': w_in.astype(_BF16),
        'lam_re1': lr1, 'lam_im1': li1,
        'lam_re8': jnp.broadcast_to(lr1, (depth, SUBLANES, N_SSM)),
        'lam_im8': jnp.broadcast_to(li1, (depth, SUBLANES, N_SSM)),
        'bmat_bf': bmat.astype(_BF16), 'cmat_bf': cmat.astype(_BF16),
        'd': row(d), 'w_glu_bf': w_glu.astype(_BF16), 'b_glu': row(b_glu),
        'w_br_ssm_bf': w_br_ssm.astype(_BF16), 'conv_w': conv_w, 'conv_w8': jnp.repeat(conv_w, SUBLANES, axis=1),
        'conv_b': row(conv_b),
        'ln_g': row(ln_g), 'ln_b': row(ln_b), 'w_br_conv_bf': w_br_conv.astype(_BF16),
        'w_br_xatt_bf': w_br_xatt.astype(_BF16), 'w_out_bf': w_out.astype(_BF16),
    }


def kernel(x_prompt, x_sample, mem_prompt, state_ssm_re, state_ssm_im, state_conv, cache_mem_k, cache_mem_v,
           norm_g, w_in, ssm_a_re, ssm_a_im, ssm_log_dt, ssm_b_re, ssm_b_im, ssm_c_re, ssm_c_im, ssm_d,
           w_glu, b_glu, w_br_ssm, conv_w, conv_b, conv_ln_g, conv_ln_b, w_br_conv,
           mem_norm_g, w_k, w_v, w_br_xatt, w_out, final_norm_g):
    bp, seq, _ = x_prompt.shape
    bs = x_sample.shape[0]
    depth = norm_g.shape[0]
    assert bp == SUBLANES and x_sample.shape[1] == 1 and depth == DEPTH
    rows, chunk, sample_blk = PROMPT_TILE_ROWS, PROMPT_CHUNK_ROWS, SAMPLE_BLOCK

    p = _prep(norm_g, w_in, ssm_a_re, ssm_a_im, ssm_log_dt, ssm_b_re, ssm_b_im, ssm_c_re, ssm_c_im, ssm_d,
              w_glu, b_glu, w_br_ssm, conv_w, conv_b, conv_ln_g, conv_ln_b, w_br_conv, w_br_xatt, w_out)
    final_g = final_norm_g.reshape(1, D_MODEL)
    wkv = jnp.concatenate([w_k, w_v], axis=-1).astype(_BF16)
    mem_k, mem_v, kb, va = _memkv(mem_prompt, mem_norm_g.reshape(depth, 1, D_MODEL), wkv)

    xp = x_prompt
    xs = x_sample.reshape(bs, D_MODEL)
    s0re = state_ssm_re.reshape(depth, bs, N_SSM)
    s0im = state_ssm_im.reshape(depth, bs, N_SSM)
    kc = cache_mem_k.reshape(depth, bs, KV_ROWS, HEAD_DIM)
    vc = cache_mem_v.reshape(depth, bs, KV_ROWS, HEAD_DIM)

    p_re, p_im, p_conv = [], [], []
    for l in range(depth):
        final = l == depth - 1
        xp, re, im, tail = _prompt_layer(xp, p, l, kb, va, final_g, rows=rows, chunk=chunk, final=final)
        p_re.append(re.reshape(bp, N_GROUPS, N_STATE))
        p_im.append(im.reshape(bp, N_GROUPS, N_STATE))
        p_conv.append(tail.reshape(CONV_K - 1, bp, E_BR).transpose(1, 0, 2))

    ys, s_re, s_im, s_conv = _sample_trunk(xs, p, s0re, s0im, state_conv.transpose(0, 2, 1, 3), kc, vc, final_g,
                                           blk=sample_blk)
    s_conv = s_conv.transpose(0, 2, 1, 3)

    y_prompt = xp
    y_sample = ys.reshape(bs, 1, D_MODEL)
    ssm_shape = (depth, bs, N_GROUPS, N_STATE)
    kv_shape = (depth, bp, N_MEM, N_HEADS, HEAD_DIM)
    return (y_prompt, y_sample, jnp.stack(p_re), jnp.stack(p_im), jnp.stack(p_conv), mem_k.reshape(kv_shape),
            mem_v.reshape(kv_shape), s_re.reshape(ssm_shape), s_im.reshape(ssm_shape), s_conv)
```

```python
import functools
import math

import jax
import jax.numpy as jnp
from jax import lax
from jax.experimental import pallas as pl
from jax.experimental.pallas import tpu as pltpu

D_MODEL = 1024
E_BR = 512
N_GROUPS = 32
GROUP = 16
N_STATE = 64
N_SSM = N_GROUPS * N_STATE
CONV_K = 31
N_HEADS = 4
HEAD_DIM = 128
N_MEM = 256
EPS = 1e-6
DEPTH = 2

LANES = 128
SUBLANES = 8
N_BLK = E_BR // LANES
BLK_STATE = N_SSM // N_BLK
KV_ROWS = N_MEM * N_HEADS

C_UA, C_ZA, C_UB, C_ZB, C_Q, C_ZX, C_GA, C_GB, C_GX, C_END = (
    0, 512, 1024, 2048, 2560, 3072, 3584, 4608, 5632, 6656)

VMEM_LIMIT = 60000 * 1024

PROMPT_TILE_ROWS = 512
PROMPT_CHUNK_ROWS = 256
CONV_PIECE_ROWS = 64
SAMPLE_BLOCK = 8
MEMKV_BLOCK = 8

_F32 = jnp.float32
_BF16 = jnp.bfloat16


def _dot(a, b):
    return jnp.dot(a, b, preferred_element_type=_F32)


def _sigmoid(x):
    return jax.nn.sigmoid(x)


def _silu(x):
    return x * _sigmoid(x)


def _rmsnorm(x, g):
    return x * lax.rsqrt(jnp.mean(x * x, axis=-1, keepdims=True) + EPS) * g


def _layernorm(x, g, b):
    mu = jnp.mean(x, axis=-1, keepdims=True)
    xc = x - mu
    var = jnp.mean(xc * xc, axis=-1, keepdims=True)
    return xc * lax.rsqrt(var + EPS) * g + b


def _layer_spec(arr, layer):
    nd = arr.ndim - 1
    return pl.BlockSpec((None,) + arr.shape[1:], lambda *_, _l=layer, _nd=nd: (_l,) + (0,) * _nd,
                        pipeline_mode=pl.Buffered(1))


def _const_spec(arr):
    nd = arr.ndim
    return pl.BlockSpec(arr.shape, lambda *_, _nd=nd: (0,) * _nd, pipeline_mode=pl.Buffered(1))


def _memkv_kernel(mem_ref, g_ref, wkv_ref, k_ref, v_ref, kb_ref, va_ref, *, n_b):
    hm = _rmsnorm(mem_ref[...].reshape(n_b * N_MEM, D_MODEL), g_ref[...]).astype(_BF16)
    kv = _dot(hm, wkv_ref[...])
    ones = jnp.ones((N_MEM, HEAD_DIM), _BF16)
    for b in range(n_b):
        kvb = kv[N_MEM * b:N_MEM * (b + 1), :]
        kb_ref[b] = kvb[:, :E_BR].astype(_BF16)
        for hh in range(N_HEADS):
            kh = kvb[:, HEAD_DIM * hh:HEAD_DIM * (hh + 1)]
            vh = kvb[:, E_BR + HEAD_DIM * hh:E_BR + HEAD_DIM * (hh + 1)]
            k_ref[b, pl.ds(hh, N_MEM, stride=N_HEADS), :] = kh
            v_ref[b, pl.ds(hh, N_MEM, stride=N_HEADS), :] = vh
            va_ref[b, hh, :, 0:HEAD_DIM] = vh.astype(_BF16)
            va_ref[b, hh, :, HEAD_DIM:2 * HEAD_DIM] = ones


def _memkv(mem, g, wkv, n_b=MEMKV_BLOCK):
    depth, bp = g.shape[0], mem.shape[0]
    kv_f = jax.ShapeDtypeStruct((depth, bp, KV_ROWS, HEAD_DIM), _F32)
    kv_spec = pl.BlockSpec((None, n_b, KV_ROWS, HEAD_DIM), lambda l, b: (l, b, 0, 0))
    return pl.pallas_call(
        functools.partial(_memkv_kernel, n_b=n_b),
        grid=(depth, bp // n_b),
        in_specs=[pl.BlockSpec((n_b, N_MEM, D_MODEL), lambda l, b: (b, 0, 0)),
                  pl.BlockSpec((None, 1, D_MODEL), lambda l, b: (l, 0, 0)),
                  pl.BlockSpec((None, D_MODEL, 2 * E_BR), lambda l, b: (l, 0, 0))],
        out_specs=[kv_spec, kv_spec,
                   pl.BlockSpec((None, n_b, N_MEM, E_BR), lambda l, b: (l, b, 0, 0)),
                   pl.BlockSpec((None, n_b, N_HEADS, N_MEM, 2 * HEAD_DIM), lambda l, b: (l, b, 0, 0, 0))],
        out_shape=[kv_f, kv_f,
                   jax.ShapeDtypeStruct((depth, bp, N_MEM, E_BR), _BF16),
                   jax.ShapeDtypeStruct((depth, bp, N_HEADS, N_MEM, 2 * HEAD_DIM), _BF16)],
        compiler_params=pltpu.CompilerParams(dimension_semantics=("arbitrary", "arbitrary"),
                                             vmem_limit_bytes=VMEM_LIMIT),
        name="mem_kv",
    )(mem, g, wkv)


def _ssm_readout(xs_bf, ua, cmat_ref, d_ref, wglu_ref, bglu_ref):
    ys = [_dot(xs_bf[:, 2 * BLK_STATE * j:2 * BLK_STATE * (j + 1)], cmat_ref[j]) for j in range(N_BLK)]
    y = jnp.concatenate(ys, axis=-1) + d_ref[...] * ua
    y = jax.nn.gelu(y, approximate=True)
    return y * _sigmoid(_dot(y.astype(_BF16), wglu_ref[...]) + bglu_ref[...])


def _zero_of(*parts):
    acc = None
    for v in parts:
        for r in range(0, v.shape[0], SUBLANES):
            for c in range(0, v.shape[1], LANES):
                t = v[r:r + SUBLANES, c:c + LANES]
                acc = t if acc is None else acc + t
    return acc * 0.0


def _after(lhs, *parts):
    acc = _zero_of(*parts)
    zero = jnp.concatenate([acc, acc], axis=0).astype(_BF16)
    top = jnp.concatenate([lhs[0:2 * SUBLANES, 0:LANES] + zero, lhs[0:2 * SUBLANES, LANES:]], axis=1)
    return jnp.concatenate([top, lhs[2 * SUBLANES:]], axis=0)


def _conv_chunk(full_s, cw8_ref, r0, chunk, n_seq, piece=CONV_PIECE_ROWS):
    out = []
    for p in range(chunk // piece):
        acc = None
        for k in range(CONV_K):
            w = cw8_ref[SUBLANES * k:SUBLANES * (k + 1), :]
            win = full_s[pl.ds(r0 + p * piece + k * n_seq, piece), :].reshape(piece // SUBLANES, SUBLANES, E_BR)
            term = win * w[None]
            acc = term if acc is None else acc + term
        out.append(acc.reshape(piece, E_BR))
    return jnp.concatenate(out, axis=0)


def _prompt_kernel(x_hbm, ng_ref, win_ref, lre_ref, lim_ref, bmat_ref, cmat_ref, d_ref, wglu_ref, bglu_ref,
                   wbra_ref, cw8_ref, cb_ref, lng_ref, lnb_ref, wbrb_ref, k_ref, va_ref, wbrx_ref, wout_ref,
                   fg_ref,
                   y_hbm, sre_ref, sim_ref, tail_ref,
                   x_buf, y_buf, sem_in, sem_out, h_s, ua_s, za_s, conv_s, gate_s, bu_s, full_s, q_s, o_s,
                   *, rows, chunk, n_seq, n_tiles, final):
    i = pl.program_id(0)
    n_t = rows // n_seq
    tail_rows = (CONV_K - 1) * n_seq
    n_chunks = rows // chunk
    t_chunk = chunk // n_seq
    scale = 1.0 / math.sqrt(HEAD_DIM)
    slot = lax.rem(i, 2)

    def in_copy(tile, slt, b):
        return pltpu.make_async_copy(x_hbm.at[b, pl.ds(tile * n_t, n_t), :], x_buf.at[slt, :, b, :],
                                     sem_in.at[slt, b])

    def out_copy(tile, slt, b):
        return pltpu.make_async_copy(y_buf.at[slt, :, b, :], y_hbm.at[b, pl.ds(tile * n_t, n_t), :],
                                     sem_out.at[slt, b])

    @pl.when(i == 0)
    def _first_fetch():
        for b in range(n_seq):
            in_copy(0, 0, b).start()

    @pl.when(i + 1 < n_tiles)
    def _prefetch():
        for b in range(n_seq):
            in_copy(i + 1, 1 - slot, b).start()

    for b in range(n_seq):
        in_copy(i, slot, b).wait()

    @pl.when(i >= 2)
    def _drain_old_output():
        for b in range(n_seq):
            out_copy(i - 2, slot, b).wait()

    @pl.when(i == 0)
    def _init():
        sre_ref[...] = jnp.zeros_like(sre_ref)
        sim_ref[...] = jnp.zeros_like(sim_ref)
        full_s[0:tail_rows, :] = jnp.zeros((tail_rows, E_BR), _F32)

    def scan_block(j, bu, rs):
        sl_s = slice(BLK_STATE * j, BLK_STATE * (j + 1))
        lam_re, lam_im = lre_ref[:, sl_s], lim_ref[:, sl_s]
        s_re, s_im = sre_ref[:, sl_s], sim_ref[:, sl_s]
        res, ims = [], []
        for t in range(t_chunk):
            b_t = bu[n_seq * t:n_seq * (t + 1), :]
            s_re, s_im = (lam_re * s_re - lam_im * s_im + b_t[:, :BLK_STATE],
                          lam_re * s_im + lam_im * s_re + b_t[:, BLK_STATE:])
            res.append(s_re)
            ims.append(s_im)
        sre_ref[:, sl_s] = s_re
        sim_ref[:, sl_s] = s_im
        bu_s[rs, 2 * BLK_STATE * j:2 * BLK_STATE * j + BLK_STATE] = jnp.concatenate(res, axis=0)
        bu_s[rs, 2 * BLK_STATE * j + BLK_STATE:2 * BLK_STATE * (j + 1)] = jnp.concatenate(ims, axis=0)
        return s_re, s_im

    def phase1(c, prev_conv):
        r0 = c * chunk
        rs = slice(r0, r0 + chunk)
        x = x_buf[slot, c * t_chunk:(c + 1) * t_chunk, :, :].reshape(chunk, D_MODEL)
        h = _rmsnorm(x, ng_ref[...]).astype(_BF16)
        h_s[rs, :] = h
        pending = list(prev_conv)

        def ordered(*state):
            return state + ((pending.pop(0),) if pending else ())

        ua = _dot(_after(h, *ordered()) if pending else h, win_ref[:, C_UA:C_ZA])
        ua_s[rs, :] = ua
        ua_bf = ua.astype(_BF16)
        drive = lambda j: _dot(ua_bf[:, LANES * j:LANES * (j + 1)], bmat_ref[j])
        b0, b1 = drive(0), drive(1)
        st = scan_block(0, b0, rs)
        q = _dot(_after(h, *ordered(*st)), win_ref[:, C_Q:C_ZX]) * scale
        for hh in range(N_HEADS):
            q_s[hh, rs, :] = q[:, HEAD_DIM * hh:HEAD_DIM * (hh + 1)]
        b2 = drive(2)
        st = scan_block(1, b1, rs)
        ub_lin = _dot(_after(h, *ordered(*st)), win_ref[:, C_UB:C_UB + E_BR])
        b3 = drive(3)
        st = scan_block(2, b2, rs)
        ub_gate = _dot(_after(h, *ordered(*st)), win_ref[:, C_UB + E_BR:C_ZB])
        full_s[tail_rows + r0:tail_rows + r0 + chunk, :] = ub_lin * _sigmoid(ub_gate)
        st = scan_block(3, b3, rs)
        za_s[rs, :] = _silu(_dot(_after(h, *st), win_ref[:, C_ZA:C_UB]))

    def conv_pieces(c):
        pieces = [_conv_chunk(full_s, cw8_ref, c * chunk + p * CONV_PIECE_ROWS, CONV_PIECE_ROWS, n_seq)
                  for p in range(chunk // CONV_PIECE_ROWS)]
        conv_s[c * chunk:(c + 1) * chunk, :] = jnp.concatenate(pieces, axis=0)
        return pieces

    prev = []
    for c in range(n_chunks):
        phase1(c, prev)
        prev = conv_pieces(c)
    h0 = h_s[0:chunk, :]
    for n, c0 in enumerate(range(C_GA, C_END, E_BR)):
        lhs = _after(h0, prev.pop(0)) if prev else h0
        gate_s[:, E_BR * n:E_BR * (n + 1)] = _sigmoid(_dot(lhs, win_ref[:, c0:c0 + E_BR]))

    pairs = [(b, hh) for b in range(n_seq) for hh in range(N_HEADS)]
    scores = []
    for b, hh in pairs:
        qb = q_s[hh, pl.ds(b, n_t, stride=n_seq), :].astype(_BF16)
        scores.append(lax.dot_general(qb, k_ref[b, :, HEAD_DIM * hh:HEAD_DIM * (hh + 1)],
                                      (((1,), (1,)), ((), ())), preferred_element_type=_F32))
    probs = [jnp.exp(s - jnp.max(s, axis=-1, keepdims=True)).astype(_BF16) for s in scores]
    outs = [_dot(p, va_ref[b, hh]) for p, (b, hh) in zip(probs, pairs)]
    for o, (b, hh) in zip(outs, pairs):
        o_s[hh, pl.ds(b, n_t, stride=n_seq), :] = o[:, :HEAD_DIM] / o[:, HEAD_DIM:]

    def phase3(c):
        r0 = c * chunk
        rs = slice(r0, r0 + chunk)
        h = h_s[rs, :]
        ya = _ssm_readout(bu_s[rs, :].astype(_BF16), ua_s[rs, :], cmat_ref, d_ref, wglu_ref, bglu_ref)
        ya = ya * za_s[rs, :]
        def gate(lo, hi):
            if c == 0:
                return gate_s[:, lo - C_GA:hi - C_GA]
            return _sigmoid(_dot(h, win_ref[:, lo:hi]))

        m = gate(C_GA, C_GB) * _dot(ya.astype(_BF16), wbra_ref[...])
        cb = _silu(_layernorm(conv_s[rs, :] + cb_ref[...], lng_ref[...], lnb_ref[...]))
        cb = cb * _silu(_dot(h, win_ref[:, C_ZB:C_Q]))
        m = m + gate(C_GB, C_GX) * _dot(cb.astype(_BF16), wbrb_ref[...])
        ox = jnp.concatenate([o_s[hh, rs, :] for hh in range(N_HEADS)], axis=-1)
        ox = ox * _silu(_dot(h, win_ref[:, C_ZX:C_GA]))
        m = m + gate(C_GX, C_END) * _dot(ox.astype(_BF16), wbrx_ref[...])
        ts = slice(c * t_chunk, (c + 1) * t_chunk)
        out = x_buf[slot, ts, :, :].reshape(chunk, D_MODEL) + _dot(m.astype(_BF16), wout_ref[...])
        if final:
            out = _rmsnorm(out, fg_ref[...])
        y_buf[slot, ts, :, :] = out.reshape(t_chunk, n_seq, D_MODEL)

    for c in range(n_chunks):
        phase3(c)

    for b in range(n_seq):
        out_copy(i, slot, b).start()

    new_tail = full_s[rows:rows + tail_rows, :]
    full_s[0:tail_rows, :] = new_tail

    @pl.when(i == n_tiles - 1)
    def _fin():
        tail_ref[...] = new_tail
        for b in range(n_seq):
            out_copy(i, slot, b).wait()
        if n_tiles >= 2:
            for b in range(n_seq):
                out_copy(i - 1, 1 - slot, b).wait()


_PROMPT_WEIGHTS = ('norm_g', 'w_in_bf', 'lam_re8', 'lam_im8', 'bmat_bf', 'cmat_bf', 'd', 'w_glu_bf', 'b_glu',
                   'w_br_ssm_bf', 'conv_w8', 'conv_b', 'ln_g', 'ln_b', 'w_br_conv_bf')


def _prompt_layer(x, p, layer, kb, va, final_g, *, rows, chunk, final):
    n_seq, seq, _ = x.shape
    n_t = rows // n_seq
    n_tiles = seq // n_t
    tail_rows = (CONV_K - 1) * n_seq
    kernel = functools.partial(_prompt_kernel, rows=rows, chunk=chunk, n_seq=n_seq, n_tiles=n_tiles, final=final)
    any_spec = pl.BlockSpec(memory_space=pl.ANY)
    stacked = [p[n] for n in _PROMPT_WEIGHTS] + [kb, va, p['w_br_xatt_bf'], p['w_out_bf']]
    state_spec = pl.BlockSpec((n_seq, N_SSM), lambda i: (0, 0))
    return pl.pallas_call(
        kernel,
        grid=(n_tiles,),
        in_specs=[any_spec] + [_layer_spec(a, layer) for a in stacked] + [_const_spec(final_g)],
        out_specs=[any_spec, state_spec, state_spec, pl.BlockSpec((tail_rows, E_BR), lambda i: (0, 0))],
        out_shape=[jax.ShapeDtypeStruct(x.shape, _F32),
                   jax.ShapeDtypeStruct((n_seq, N_SSM), _F32),
                   jax.ShapeDtypeStruct((n_seq, N_SSM), _F32),
                   jax.ShapeDtypeStruct((tail_rows, E_BR), _F32)],
        scratch_shapes=[pltpu.VMEM((2, n_t, n_seq, D_MODEL), _F32),
                        pltpu.VMEM((2, n_t, n_seq, D_MODEL), _F32),
                        pltpu.SemaphoreType.DMA((2, n_seq)),
                        pltpu.SemaphoreType.DMA((2, n_seq)),
                        pltpu.VMEM((rows, D_MODEL), _BF16),
                        pltpu.VMEM((rows, E_BR), _F32),
                        pltpu.VMEM((rows, E_BR), _F32),
                        pltpu.VMEM((rows, E_BR), _F32),
                        pltpu.VMEM((chunk, C_END - C_GA), _F32),
                        pltpu.VMEM((rows, 2 * N_SSM), _F32),
                        pltpu.VMEM((tail_rows + rows, E_BR), _F32),
                        pltpu.VMEM((N_HEADS, rows, HEAD_DIM), _F32),
                        pltpu.VMEM((N_HEADS, rows, HEAD_DIM), _F32)],
        compiler_params=pltpu.CompilerParams(dimension_semantics=("arbitrary",), vmem_limit_bytes=VMEM_LIMIT),
        name="prompt_layer",
    )(x, *stacked, final_g)


def _sample_kernel(x_ref, ng_ref, win_ref, lre_ref, lim_ref, bmat_ref, cmat_ref, d_ref,
                   wglu_ref, bglu_ref, wbra_ref, cw_ref, cb_ref, lng_ref, lnb_ref, wbrb_ref,
                   s0re_ref, s0im_ref, cst_ref, k_ref, v_ref, wbrx_ref, wout_ref, fg_ref,
                   y_ref, sre_ref, sim_ref, cnew_ref,
                   x_s, h_s, m_s, q_s, o_s, vb_s, acc_s,
                   *, n, blk):
    layer = pl.program_id(0)
    i = pl.program_id(1)
    scale = 1.0 / math.sqrt(HEAD_DIM)
    groups = KV_ROWS // SUBLANES

    @pl.when((layer == 0) & (i == 0))
    def _load_x():
        x_s[...] = x_ref[...]

    @pl.when(i == 0)
    def _pre():
        h = _rmsnorm(x_s[...], ng_ref[...]).astype(_BF16)
        h_s[...] = h
        ua = _dot(h, win_ref[:, C_UA:C_ZA])
        ua_bf = ua.astype(_BF16)
        xs_parts = []
        for j in range(N_BLK):
            sl_s = slice(BLK_STATE * j, BLK_STATE * (j + 1))
            bu = _dot(ua_bf[:, LANES * j:LANES * (j + 1)], bmat_ref[j])
            l_re, l_im = lre_ref[:, sl_s], lim_ref[:, sl_s]
            o_re, o_im = s0re_ref[:, sl_s], s0im_ref[:, sl_s]
            n_re = l_re * o_re - l_im * o_im + bu[:, :BLK_STATE]
            n_im = l_re * o_im + l_im * o_re + bu[:, BLK_STATE:]
            sre_ref[:, sl_s] = n_re
            sim_ref[:, sl_s] = n_im
            xs_parts += [n_re, n_im]
        xs_bf = jnp.concatenate(xs_parts, axis=-1).astype(_BF16)
        ya = _ssm_readout(xs_bf, ua, cmat_ref, d_ref, wglu_ref, bglu_ref)
        ya = ya * _silu(_dot(h, win_ref[:, C_ZA:C_UB]))
        m_s[...] = _sigmoid(_dot(h, win_ref[:, C_GA:C_GB])) * _dot(ya.astype(_BF16), wbra_ref[...])
        ub = _dot(h, win_ref[:, C_UB:C_ZB])
        vb_s[...] = ub[:, :E_BR] * _sigmoid(ub[:, E_BR:])
        q = _dot(h, win_ref[:, C_Q:C_ZX]) * scale
        for j in range(SUBLANES):
            hh = j % N_HEADS
            q_s[n * j:n * (j + 1), :] = q[:, HEAD_DIM * hh:HEAD_DIM * (hh + 1)]

    ones = jnp.ones((HEAD_DIM, HEAD_DIM), _BF16)

    def fold(t):
        return t, pltpu.roll(t, N_HEADS, axis=0)

    for s in range(blk):
        seq = i * blk + s
        row = pl.ds(seq, 1)
        q8 = q_s[pl.ds(seq, SUBLANES, stride=n), :]
        kq = (k_ref[s].reshape(groups, SUBLANES, HEAD_DIM) * q8[None]).astype(_BF16)
        sc = _dot(kq.reshape(KV_ROWS, HEAD_DIM), ones).reshape(groups, SUBLANES, HEAD_DIM)
        a, b2 = fold(jnp.max(sc, axis=0))
        e = jnp.exp(sc - jnp.maximum(a, b2)[None])
        a, b2 = fold(jnp.sum(e, axis=0))
        den = a + b2
        a, b2 = fold(jnp.sum(e * v_ref[s].reshape(groups, SUBLANES, HEAD_DIM), axis=0))
        o8 = (a + b2) / den
        for hh in range(N_HEADS):
            o_s[hh, row, :] = o8[hh:hh + 1, :]

    rows = pl.ds(pl.multiple_of(i * blk, blk), blk)
    acc = cst_ref[0] * cw_ref[0:1, :]
    for k in range(1, CONV_K - 1):
        acc = acc + cst_ref[k] * cw_ref[k:k + 1, :]
    acc_s[rows, :] = acc
    cnew_ref[0:CONV_K - 2] = cst_ref[1:CONV_K - 1]
    cnew_ref[CONV_K - 2] = vb_s[rows, :]

    @pl.when(i == pl.num_programs(1) - 1)
    def _post():
        h = h_s[...]
        acc = acc_s[...] + cb_ref[...] + cw_ref[CONV_K - 1:CONV_K, :] * vb_s[...]
        cb = _silu(_layernorm(acc, lng_ref[...], lnb_ref[...]))
        cb = cb * _silu(_dot(h, win_ref[:, C_ZB:C_Q]))
        m = m_s[...] + _sigmoid(_dot(h, win_ref[:, C_GB:C_GX])) * _dot(cb.astype(_BF16), wbrb_ref[...])
        ox = jnp.concatenate([o_s[hh] for hh in range(N_HEADS)], axis=-1)
        ox = ox * _silu(_dot(h, win_ref[:, C_ZX:C_GA]))
        m = m + _sigmoid(_dot(h, win_ref[:, C_GX:C_END])) * _dot(ox.astype(_BF16), wbrx_ref[...])
        out = x_s[...] + _dot(m.astype(_BF16), wout_ref[...])
        x_s[...] = out

        @pl.when(layer == pl.num_programs(0) - 1)
        def _final():
            y_ref[...] = _rmsnorm(out, fg_ref[...])


def _sample_trunk(x2d, p, s0re, s0im, cst, kc, vc, final_g, *, blk):
    n = x2d.shape[0]
    depth = cst.shape[0]
    kernel = functools.partial(_sample_kernel, n=n, blk=blk)

    def per_layer(arr):
        nd = arr.ndim - 1
        return pl.BlockSpec((None,) + arr.shape[1:], lambda l, i, _nd=nd: (l,) + (0,) * _nd,
                            pipeline_mode=pl.Buffered(1))

    kv_spec = pl.BlockSpec((None, blk, KV_ROWS, HEAD_DIM), lambda l, i: (l, i, 0, 0))
    win_spec = pl.BlockSpec((None, CONV_K - 1, blk, E_BR), lambda l, i: (l, 0, i, 0))
    mid = [p[nm] for nm in ('lam_re1', 'lam_im1', 'bmat_bf', 'cmat_bf', 'd', 'w_glu_bf', 'b_glu', 'w_br_ssm_bf',
                            'conv_w', 'conv_b', 'ln_g', 'ln_b', 'w_br_conv_bf')]
    ins = [x2d, p['norm_g'], p['w_in_bf']] + mid + [s0re, s0im, cst, kc, vc, p['w_br_xatt_bf'], p['w_out_bf'],
                                                    final_g]
    in_specs = ([_const_spec(x2d), per_layer(p['norm_g']), per_layer(p['w_in_bf'])] + [per_layer(a) for a in mid]
                + [per_layer(s0re), per_layer(s0im), win_spec, kv_spec, kv_spec, per_layer(p['w_br_xatt_bf']),
                   per_layer(p['w_out_bf']), _const_spec(final_g)])
    state_spec = pl.BlockSpec((None, n, N_SSM), lambda l, i: (l, 0, 0))
    return pl.pallas_call(
        kernel,
        grid=(depth, n // blk),
        in_specs=in_specs,
        out_specs=[pl.BlockSpec((n, D_MODEL), lambda l, i: (0, 0)), state_spec, state_spec, win_spec],
        out_shape=[jax.ShapeDtypeStruct((n, D_MODEL), _F32),
                   jax.ShapeDtypeStruct((depth, n, N_SSM), _F32),
                   jax.ShapeDtypeStruct((depth, n, N_SSM), _F32),
                   jax.ShapeDtypeStruct(cst.shape, _F32)],
        scratch_shapes=[pltpu.VMEM((n, D_MODEL), _F32),
                        pltpu.VMEM((n, D_MODEL), _BF16),
                        pltpu.VMEM((n, D_MODEL), _F32),
                        pltpu.VMEM((SUBLANES * n, HEAD_DIM), _F32),
                        pltpu.VMEM((N_HEADS, n, HEAD_DIM), _F32),
                        pltpu.VMEM((n, E_BR), _F32),
                        pltpu.VMEM((n, E_BR), _F32)],
        compiler_params=pltpu.CompilerParams(dimension_semantics=("arbitrary", "arbitrary"),
                                             vmem_limit_bytes=VMEM_LIMIT),
        name="sample_trunk",
    )(*ins)


def _prep(norm_g, w_in, a_re, a_im, log_dt, b_re, b_im, c_re, c_im, d, w_glu, b_glu, w_br_ssm,
          conv_w, conv_b, ln_g, ln_b, w_br_conv, w_br_xatt, w_out):
    depth = norm_g.shape[0]
    dt = jnp.exp(log_dt)[..., None]
    mag = jnp.exp(a_re * dt)
    lr = mag * jnp.cos(a_im * dt)
    li = mag * jnp.sin(a_im * dt)
    den = a_re * a_re + a_im * a_im
    cr = ((lr - 1.0) * a_re + li * a_im) / den
    ci = (li * a_re - (lr - 1.0) * a_im) / den
    bb_re = cr[..., None] * b_re - ci[..., None] * b_im
    bb_im = cr[..., None] * b_im + ci[..., None] * b_re
    gpb = N_GROUPS // N_BLK
    eye = jnp.eye(gpb, dtype=_F32)

    def b_blocks(bb):
        bb = bb.reshape(depth, N_BLK, gpb, N_STATE, GROUP)
        return jnp.einsum('ljgph,gk->ljghkp', bb, eye).reshape(depth, N_BLK, LANES, BLK_STATE)

    def c_blocks(cc):
        cc = cc.reshape(depth, N_BLK, gpb, GROUP, N_STATE)
        return jnp.einsum('ljghp,gk->ljkpgh', cc, eye).reshape(depth, N_BLK, BLK_STATE, LANES)

    bmat = jnp.concatenate([b_blocks(bb_re), b_blocks(bb_im)], axis=-1)
    cmat = jnp.concatenate([c_blocks(c_re), -c_blocks(c_im)], axis=2)
    lr1 = lr.reshape(depth, 1, N_SSM)
    li1 = li.reshape(depth, 1, N_SSM)
    row = lambda v: v.reshape(depth, 1, -1)
    return {
        'norm_g': row(norm_g), 'w_in_bf': w_in.astype(_BF16),
        'lam_re1': lr1, 'lam_im1': li1,
        'lam_re8': jnp.broadcast_to(lr1, (depth, SUBLANES, N_SSM)),
        'lam_im8': jnp.broadcast_to(li1, (depth, SUBLANES, N_SSM)),
        'bmat_bf': bmat.astype(_BF16), 'cmat_bf': cmat.astype(_BF16),
        'd': row(d), 'w_glu_bf': w_glu.astype(_BF16), 'b_glu': row(b_glu),
        'w_br_ssm_bf': w_br_ssm.astype(_BF16), 'conv_w': conv_w, 'conv_w8': jnp.repeat(conv_w, SUBLANES, axis=1),
        'conv_b': row(conv_b),
        'ln_g': row(ln_g), 'ln_b': row(ln_b), 'w_br_conv_bf': w_br_conv.astype(_BF16),
        'w_br_xatt_bf': w_br_xatt.astype(_BF16), 'w_out_bf': w_out.astype(_BF16),
    }


def kernel(x_prompt, x_sample, mem_prompt, state_ssm_re, state_ssm_im, state_conv, cache_mem_k, cache_mem_v,
           norm_g, w_in, ssm_a_re, ssm_a_im, ssm_log_dt, ssm_b_re, ssm_b_im, ssm_c_re, ssm_c_im, ssm_d,
           w_glu, b_glu, w_br_ssm, conv_w, conv_b, conv_ln_g, conv_ln_b, w_br_conv,
           mem_norm_g, w_k, w_v, w_br_xatt, w_out, final_norm_g):
    bp, seq, _ = x_prompt.shape
    bs = x_sample.shape[0]
    depth = norm_g.shape[0]
    assert bp == SUBLANES and x_sample.shape[1] == 1 and depth == DEPTH
    rows, chunk, sample_blk = PROMPT_TILE_ROWS, PROMPT_CHUNK_ROWS, SAMPLE_BLOCK

    p = _prep(norm_g, w_in, ssm_a_re, ssm_a_im, ssm_log_dt, ssm_b_re, ssm_b_im, ssm_c_re, ssm_c_im, ssm_d,
              w_glu, b_glu, w_br_ssm, conv_w, conv_b, conv_ln_g, conv_ln_b, w_br_conv, w_br_xatt, w_out)
    final_g = final_norm_g.reshape(1, D_MODEL)
    wkv = jnp.concatenate([w_k, w_v], axis=-1).astype(_BF16)
    mem_k, mem_v, kb, va = _memkv(mem_prompt, mem_norm_g.reshape(depth, 1, D_MODEL), wkv)

    xp = x_prompt
    xs = x_sample.reshape(bs, D_MODEL)
    s0re = state_ssm_re.reshape(depth, bs, N_SSM)
    s0im = state_ssm_im.reshape(depth, bs, N_SSM)
    kc = cache_mem_k.reshape(depth, bs, KV_ROWS, HEAD_DIM)
    vc = cache_mem_v.reshape(depth, bs, KV_ROWS, HEAD_DIM)

    p_re, p_im, p_conv = [], [], []
    for l in range(depth):
        final = l == depth - 1
        xp, re, im, tail = _prompt_layer(xp, p, l, kb, va, final_g, rows=rows, chunk=chunk, final=final)
        p_re.append(re.reshape(bp, N_GROUPS, N_STATE))
        p_im.append(im.reshape(bp, N_GROUPS, N_STATE))
        p_conv.append(tail.reshape(CONV_K - 1, bp, E_BR).transpose(1, 0, 2))

    ys, s_re, s_im, s_conv = _sample_trunk(xs, p, s0re, s0im, state_conv.transpose(0, 2, 1, 3), kc, vc, final_g,
                                           blk=sample_blk)
    s_conv = s_conv.transpose(0, 2, 1, 3)

    y_prompt = xp
    y_sample = ys.reshape(bs, 1, D_MODEL)
    ssm_shape = (depth, bs, N_GROUPS, N_STATE)
    kv_shape = (depth, bp, N_MEM, N_HEADS, HEAD_DIM)
    return (y_prompt, y_sample, jnp.stack(p_re), jnp.stack(p_im), jnp.stack(p_conv), mem_k.reshape(kv_shape),
            mem_v.reshape(kv_shape), s_re.reshape(ssm_shape), s_im.reshape(ssm_shape), s_conv)
```

```python
import functools
import math

import jax
import jax.numpy as jnp
from jax import lax
from jax.experimental import pallas as pl
from jax.experimental.pallas import tpu as pltpu

D_MODEL = 1024
E_BR = 512
N_GROUPS = 32
GROUP = 16
N_STATE = 64
N_SSM = N_GROUPS * N_STATE
CONV_K = 31
N_HEADS = 4
HEAD_DIM = 128
N_MEM = 256
EPS = 1e-6
DEPTH = 2

LANES = 128
SUBLANES = 8
N_BLK = E_BR // LANES
BLK_STATE = N_SSM // N_BLK
KV_ROWS = N_MEM * N_HEADS

C_UA, C_ZA, C_UB, C_ZB, C_Q, C_ZX, C_GA, C_GB, C_GX, C_END = (
    0, 512, 1024, 2048, 2560, 3072, 3584, 4608, 5632, 6656)

VMEM_LIMIT = 60000 * 1024

PROMPT_TILE_ROWS = 512
PROMPT_CHUNK_ROWS = 256
CONV_PIECE_ROWS = 64
SAMPLE_BLOCK = 8
MEMKV_BLOCK = 8

_F32 = jnp.float32
_BF16 = jnp.bfloat16


def _dot(a, b):
    return jnp.dot(a, b, preferred_element_type=_F32)


def _sigmoid(x):
    return jax.nn.sigmoid(x)


def _silu(x):
    return x * _sigmoid(x)


def _rmsnorm(x, g):
    return x * lax.rsqrt(jnp.mean(x * x, axis=-1, keepdims=True) + EPS) * g


def _layernorm(x, g, b):
    mu = jnp.mean(x, axis=-1, keepdims=True)
    xc = x - mu
    var = jnp.mean(xc * xc, axis=-1, keepdims=True)
    return xc * lax.rsqrt(var + EPS) * g + b


def _layer_spec(arr, layer):
    nd = arr.ndim - 1
    return pl.BlockSpec((None,) + arr.shape[1:], lambda *_, _l=layer, _nd=nd: (_l,) + (0,) * _nd,
                        pipeline_mode=pl.Buffered(1))


def _const_spec(arr):
    nd = arr.ndim
    return pl.BlockSpec(arr.shape, lambda *_, _nd=nd: (0,) * _nd, pipeline_mode=pl.Buffered(1))


def _memkv_kernel(mem_ref, g_ref, wkv_ref, k_ref, v_ref, kb_ref, va_ref, *, n_b):
    hm = _rmsnorm(mem_ref[...].reshape(n_b * N_MEM, D_MODEL), g_ref[...]).astype(_BF16)
    kv = _dot(hm, wkv_ref[...])
    ones = jnp.ones((N_MEM, HEAD_DIM), _BF16)
    for b in range(n_b):
        kvb = kv[N_MEM * b:N_MEM * (b + 1), :]
        kb_ref[b] = kvb[:, :E_BR].astype(_BF16)
        for hh in range(N_HEADS):
            kh = kvb[:, HEAD_DIM * hh:HEAD_DIM * (hh + 1)]
            vh = kvb[:, E_BR + HEAD_DIM * hh:E_BR + HEAD_DIM * (hh + 1)]
            k_ref[b, pl.ds(hh, N_MEM, stride=N_HEADS), :] = kh
            v_ref[b, pl.ds(hh, N_MEM, stride=N_HEADS), :] = vh
            va_ref[b, hh, :, 0:HEAD_DIM] = vh.astype(_BF16)
            va_ref[b, hh, :, HEAD_DIM:2 * HEAD_DIM] = ones


def _memkv(mem, g, wkv, n_b=MEMKV_BLOCK):
    depth, bp = g.shape[0], mem.shape[0]
    kv_f = jax.ShapeDtypeStruct((depth, bp, KV_ROWS, HEAD_DIM), _F32)
    kv_spec = pl.BlockSpec((None, n_b, KV_ROWS, HEAD_DIM), lambda l, b: (l, b, 0, 0))
    return pl.pallas_call(
        functools.partial(_memkv_kernel, n_b=n_b),
        grid=(depth, bp // n_b),
        in_specs=[pl.BlockSpec((n_b, N_MEM, D_MODEL), lambda l, b: (b, 0, 0)),
                  pl.BlockSpec((None, 1, D_MODEL), lambda l, b: (l, 0, 0)),
                  pl.BlockSpec((None, D_MODEL, 2 * E_BR), lambda l, b: (l, 0, 0))],
        out_specs=[kv_spec, kv_spec,
                   pl.BlockSpec((None, n_b, N_MEM, E_BR), lambda l, b: (l, b, 0, 0)),
                   pl.BlockSpec((None, n_b, N_HEADS, N_MEM, 2 * HEAD_DIM), lambda l, b: (l, b, 0, 0, 0))],
        out_shape=[kv_f, kv_f,
                   jax.ShapeDtypeStruct((depth, bp, N_MEM, E_BR), _BF16),
                   jax.ShapeDtypeStruct((depth, bp, N_HEADS, N_MEM, 2 * HEAD_DIM), _BF16)],
        compiler_params=pltpu.CompilerParams(dimension_semantics=("arbitrary", "arbitrary"),
                                             vmem_limit_bytes=VMEM_LIMIT),
        name="mem_kv",
    )(mem, g, wkv)


def _ssm_readout(xs_bf, ua, cmat_ref, d_ref, wglu_ref, bglu_ref):
    ys = [_dot(xs_bf[:, 2 * BLK_STATE * j:2 * BLK_STATE * (j + 1)], cmat_ref[j]) for j in range(N_BLK)]
    y = jnp.concatenate(ys, axis=-1) + d_ref[...] * ua
    y = jax.nn.gelu(y, approximate=True)
    return y * _sigmoid(_dot(y.astype(_BF16), wglu_ref[...]) + bglu_ref[...])


def _zero_of(*parts):
    acc = None
    for v in parts:
        for r in range(0, v.shape[0], SUBLANES):
            for c in range(0, v.shape[1], LANES):
                t = v[r:r + SUBLANES, c:c + LANES]
                acc = t if acc is None else acc + t
    return acc * 0.0


def _after(lhs, *parts):
    acc = _zero_of(*parts)
    zero = jnp.concatenate([acc, acc], axis=0).astype(_BF16)
    top = jnp.concatenate([lhs[0:2 * SUBLANES, 0:LANES] + zero, lhs[0:2 * SUBLANES, LANES:]], axis=1)
    return jnp.concatenate([top, lhs[2 * SUBLANES:]], axis=0)


def _conv_chunk(full_s, cw8_ref, r0, chunk, n_seq, piece=CONV_PIECE_ROWS):
    out = []
    for p in range(chunk // piece):
        acc = None
        for k in range(CONV_K):
            w = cw8_ref[SUBLANES * k:SUBLANES * (k + 1), :]
            win = full_s[pl.ds(r0 + p * piece + k * n_seq, piece), :].reshape(piece // SUBLANES, SUBLANES, E_BR)
            term = win * w[None]
            acc = term if acc is None else acc + term
        out.append(acc.reshape(piece, E_BR))
    return jnp.concatenate(out, axis=0)


def _prompt_kernel(x_hbm, ng_ref, win_ref, lre_ref, lim_ref, bmat_ref, cmat_ref, d_ref, wglu_ref, bglu_ref,
                   wbra_ref, cw8_ref, cb_ref, lng_ref, lnb_ref, wbrb_ref, k_ref, va_ref, wbrx_ref, wout_ref,
                   fg_ref,
                   y_hbm, sre_ref, sim_ref, tail_ref,
                   x_buf, y_buf, sem_in, sem_out, h_s, ua_s, za_s, conv_s, bu_s, full_s, q_s, o_s,
                   *, rows, chunk, n_seq, n_tiles, final):
    i = pl.program_id(0)
    n_t = rows // n_seq
    tail_rows = (CONV_K - 1) * n_seq
    n_chunks = rows // chunk
    t_chunk = chunk // n_seq
    scale = 1.0 / math.sqrt(HEAD_DIM)
    slot = lax.rem(i, 2)

    def in_copy(tile, slt, b):
        return pltpu.make_async_copy(x_hbm.at[b, pl.ds(tile * n_t, n_t), :], x_buf.at[slt, :, b, :],
                                     sem_in.at[slt, b])

    def out_copy(tile, slt, b):
        return pltpu.make_async_copy(y_buf.at[slt, :, b, :], y_hbm.at[b, pl.ds(tile * n_t, n_t), :],
                                     sem_out.at[slt, b])

    @pl.when(i == 0)
    def _first_fetch():
        for b in range(n_seq):
            in_copy(0, 0, b).start()

    @pl.when(i + 1 < n_tiles)
    def _prefetch():
        for b in range(n_seq):
            in_copy(i + 1, 1 - slot, b).start()

    for b in range(n_seq):
        in_copy(i, slot, b).wait()

    @pl.when(i >= 2)
    def _drain_old_output():
        for b in range(n_seq):
            out_copy(i - 2, slot, b).wait()

    @pl.when(i == 0)
    def _init():
        sre_ref[...] = jnp.zeros_like(sre_ref)
        sim_ref[...] = jnp.zeros_like(sim_ref)
        full_s[0:tail_rows, :] = jnp.zeros((tail_rows, E_BR), _F32)

    def scan_block(j, bu, rs):
        sl_s = slice(BLK_STATE * j, BLK_STATE * (j + 1))
        lam_re, lam_im = lre_ref[:, sl_s], lim_ref[:, sl_s]
        s_re, s_im = sre_ref[:, sl_s], sim_ref[:, sl_s]
        res, ims = [], []
        for t in range(t_chunk):
            b_t = bu[n_seq * t:n_seq * (t + 1), :]
            s_re, s_im = (lam_re * s_re - lam_im * s_im + b_t[:, :BLK_STATE],
                          lam_re * s_im + lam_im * s_re + b_t[:, BLK_STATE:])
            res.append(s_re)
            ims.append(s_im)
        sre_ref[:, sl_s] = s_re
        sim_ref[:, sl_s] = s_im
        bu_s[rs, 2 * BLK_STATE * j:2 * BLK_STATE * j + BLK_STATE] = jnp.concatenate(res, axis=0)
        bu_s[rs, 2 * BLK_STATE * j + BLK_STATE:2 * BLK_STATE * (j + 1)] = jnp.concatenate(ims, axis=0)
        return s_re, s_im

    def phase1(c, prev_conv):
        r0 = c * chunk
        rs = slice(r0, r0 + chunk)
        x = x_buf[slot, c * t_chunk:(c + 1) * t_chunk, :, :].reshape(chunk, D_MODEL)
        h = _rmsnorm(x, ng_ref[...]).astype(_BF16)
        h_s[rs, :] = h
        pending = list(prev_conv)

        def ordered(*state):
            return state + ((pending.pop(0),) if pending else ())

        ua = _dot(_after(h, *ordered()) if pending else h, win_ref[:, C_UA:C_ZA])
        ua_s[rs, :] = ua
        ua_bf = ua.astype(_BF16)
        drive = lambda j: _dot(ua_bf[:, LANES * j:LANES * (j + 1)], bmat_ref[j])
        b0, b1 = drive(0), drive(1)
        st = scan_block(0, b0, rs)
        q = _dot(_after(h, *ordered(*st)), win_ref[:, C_Q:C_ZX]) * scale
        for hh in range(N_HEADS):
            q_s[hh, rs, :] = q[:, HEAD_DIM * hh:HEAD_DIM * (hh + 1)]
        b2 = drive(2)
        st = scan_block(1, b1, rs)
        ub_lin = _dot(_after(h, *ordered(*st)), win_ref[:, C_UB:C_UB + E_BR])
        b3 = drive(3)
        st = scan_block(2, b2, rs)
        ub_gate = _dot(_after(h, *ordered(*st)), win_ref[:, C_UB + E_BR:C_ZB])
        full_s[tail_rows + r0:tail_rows + r0 + chunk, :] = ub_lin * _sigmoid(ub_gate)
        st = scan_block(3, b3, rs)
        za_s[rs, :] = _silu(_dot(_after(h, *st), win_ref[:, C_ZA:C_UB]))

    def conv_pieces(c):
        pieces = [_conv_chunk(full_s, cw8_ref, c * chunk + p * CONV_PIECE_ROWS, CONV_PIECE_ROWS, n_seq)
                  for p in range(chunk // CONV_PIECE_ROWS)]
        conv_s[c * chunk:(c + 1) * chunk, :] = jnp.concatenate(pieces, axis=0)
        return pieces

    prev = []
    for c in range(n_chunks):
        phase1(c, prev)
        prev = conv_pieces(c) if c + 1 < n_chunks else []

    pairs = [(b, hh) for b in range(n_seq) for hh in range(N_HEADS)]
    scores = []
    for b, hh in pairs:
        qb = q_s[hh, pl.ds(b, n_t, stride=n_seq), :].astype(_BF16)
        scores.append(lax.dot_general(qb, k_ref[b, :, HEAD_DIM * hh:HEAD_DIM * (hh + 1)],
                                      (((1,), (1,)), ((), ())), preferred_element_type=_F32))
    probs = [jnp.exp(s - jnp.max(s, axis=-1, keepdims=True)).astype(_BF16) for s in scores]
    outs = [_dot(p, va_ref[b, hh]) for p, (b, hh) in zip(probs, pairs)]
    for o, (b, hh) in zip(outs, pairs):
        o_s[hh, pl.ds(b, n_t, stride=n_seq), :] = o[:, :HEAD_DIM] / o[:, HEAD_DIM:]

    def phase3(c):
        r0 = c * chunk
        rs = slice(r0, r0 + chunk)
        h = h_s[rs, :]
        ya = _ssm_readout(bu_s[rs, :].astype(_BF16), ua_s[rs, :], cmat_ref, d_ref, wglu_ref, bglu_ref)
        ya = ya * za_s[rs, :]
        m = _sigmoid(_dot(h, win_ref[:, C_GA:C_GB])) * _dot(ya.astype(_BF16), wbra_ref[...])
        conv = conv_s[rs, :] if c + 1 < n_chunks else _conv_chunk(full_s, cw8_ref, r0, chunk, n_seq)
        acc = conv + cb_ref[...]
        cb = _silu(_layernorm(acc, lng_ref[...], lnb_ref[...]))
        cb = cb * _silu(_dot(h, win_ref[:, C_ZB:C_Q]))
        m = m + _sigmoid(_dot(h, win_ref[:, C_GB:C_GX])) * _dot(cb.astype(_BF16), wbrb_ref[...])
        ox = jnp.concatenate([o_s[hh, rs, :] for hh in range(N_HEADS)], axis=-1)
        ox = ox * _silu(_dot(h, win_ref[:, C_ZX:C_GA]))
        m = m + _sigmoid(_dot(h, win_ref[:, C_GX:C_END])) * _dot(ox.astype(_BF16), wbrx_ref[...])
        ts = slice(c * t_chunk, (c + 1) * t_chunk)
        out = x_buf[slot, ts, :, :].reshape(chunk, D_MODEL) + _dot(m.astype(_BF16), wout_ref[...])
        if final:
            out = _rmsnorm(out, fg_ref[...])
        y_buf[slot, ts, :, :] = out.reshape(t_chunk, n_seq, D_MODEL)

    for c in reversed(range(n_chunks)):
        phase3(c)

    for b in range(n_seq):
        out_copy(i, slot, b).start()

    new_tail = full_s[rows:rows + tail_rows, :]
    full_s[0:tail_rows, :] = new_tail

    @pl.when(i == n_tiles - 1)
    def _fin():
        tail_ref[...] = new_tail
        for b in range(n_seq):
            out_copy(i, slot, b).wait()
        if n_tiles >= 2:
            for b in range(n_seq):
                out_copy(i - 1, 1 - slot, b).wait()


_PROMPT_WEIGHTS = ('norm_g', 'w_in_bf', 'lam_re8', 'lam_im8', 'bmat_bf', 'cmat_bf', 'd', 'w_glu_bf', 'b_glu',
                   'w_br_ssm_bf', 'conv_w8', 'conv_b', 'ln_g', 'ln_b', 'w_br_conv_bf')


def _prompt_layer(x, p, layer, kb, va, final_g, *, rows, chunk, final):
    n_seq, seq, _ = x.shape
    n_t = rows // n_seq
    n_tiles = seq // n_t
    tail_rows = (CONV_K - 1) * n_seq
    kernel = functools.partial(_prompt_kernel, rows=rows, chunk=chunk, n_seq=n_seq, n_tiles=n_tiles, final=final)
    any_spec = pl.BlockSpec(memory_space=pl.ANY)
    stacked = [p[n] for n in _PROMPT_WEIGHTS] + [kb, va, p['w_br_xatt_bf'], p['w_out_bf']]
    state_spec = pl.BlockSpec((n_seq, N_SSM), lambda i: (0, 0))
    return pl.pallas_call(
        kernel,
        grid=(n_tiles,),
        in_specs=[any_spec] + [_layer_spec(a, layer) for a in stacked] + [_const_spec(final_g)],
        out_specs=[any_spec, state_spec, state_spec, pl.BlockSpec((tail_rows, E_BR), lambda i: (0, 0))],
        out_shape=[jax.ShapeDtypeStruct(x.shape, _F32),
                   jax.ShapeDtypeStruct((n_seq, N_SSM), _F32),
                   jax.ShapeDtypeStruct((n_seq, N_SSM), _F32),
                   jax.ShapeDtypeStruct((tail_rows, E_BR), _F32)],
        scratch_shapes=[pltpu.VMEM((2, n_t, n_seq, D_MODEL), _F32),
                        pltpu.VMEM((2, n_t, n_seq, D_MODEL), _F32),
                        pltpu.SemaphoreType.DMA((2, n_seq)),
                        pltpu.SemaphoreType.DMA((2, n_seq)),
                        pltpu.VMEM((rows, D_MODEL), _BF16),
                        pltpu.VMEM((rows, E_BR), _F32),
                        pltpu.VMEM((rows, E_BR), _F32),
                        pltpu.VMEM((rows, E_BR), _F32),
                        pltpu.VMEM((rows, 2 * N_SSM), _F32),
                        pltpu.VMEM((tail_rows + rows, E_BR), _F32),
                        pltpu.VMEM((N_HEADS, rows, HEAD_DIM), _F32),
                        pltpu.VMEM((N_HEADS, rows, HEAD_DIM), _F32)],
        compiler_params=pltpu.CompilerParams(dimension_semantics=("arbitrary",), vmem_limit_bytes=VMEM_LIMIT),
        name="prompt_layer",
    )(x, *stacked, final_g)


def _sample_kernel(x_ref, ng_ref, win_ref, lre_ref, lim_ref, bmat_ref, cmat_ref, d_ref,
                   wglu_ref, bglu_ref, wbra_ref, cw_ref, cb_ref, lng_ref, lnb_ref, wbrb_ref,
                   s0re_ref, s0im_ref, cst_ref, k_ref, v_ref, wbrx_ref, wout_ref, fg_ref,
                   y_ref, sre_ref, sim_ref, cnew_ref,
                   x_s, h_s, m_s, q_s, o_s, vb_s, acc_s,
                   *, n, blk):
    layer = pl.program_id(0)
    i = pl.program_id(1)
    scale = 1.0 / math.sqrt(HEAD_DIM)
    groups = KV_ROWS // SUBLANES

    @pl.when((layer == 0) & (i == 0))
    def _load_x():
        x_s[...] = x_ref[...]

    @pl.when(i == 0)
    def _pre():
        h = _rmsnorm(x_s[...], ng_ref[...]).astype(_BF16)
        h_s[...] = h
        ua = _dot(h, win_ref[:, C_UA:C_ZA])
        ua_bf = ua.astype(_BF16)
        xs_parts = []
        for j in range(N_BLK):
            sl_s = slice(BLK_STATE * j, BLK_STATE * (j + 1))
            bu = _dot(ua_bf[:, LANES * j:LANES * (j + 1)], bmat_ref[j])
            l_re, l_im = lre_ref[:, sl_s], lim_ref[:, sl_s]
            o_re, o_im = s0re_ref[:, sl_s], s0im_ref[:, sl_s]
            n_re = l_re * o_re - l_im * o_im + bu[:, :BLK_STATE]
            n_im = l_re * o_im + l_im * o_re + bu[:, BLK_STATE:]
            sre_ref[:, sl_s] = n_re
            sim_ref[:, sl_s] = n_im
            xs_parts += [n_re, n_im]
        xs_bf = jnp.concatenate(xs_parts, axis=-1).astype(_BF16)
        ya = _ssm_readout(xs_bf, ua, cmat_ref, d_ref, wglu_ref, bglu_ref)
        ya = ya * _silu(_dot(h, win_ref[:, C_ZA:C_UB]))
        m_s[...] = _sigmoid(_dot(h, win_ref[:, C_GA:C_GB])) * _dot(ya.astype(_BF16), wbra_ref[...])
        ub = _dot(h, win_ref[:, C_UB:C_ZB])
        vb_s[...] = ub[:, :E_BR] * _sigmoid(ub[:, E_BR:])
        q = _dot(h, win_ref[:, C_Q:C_ZX]) * scale
        for j in range(SUBLANES):
            hh = j % N_HEADS
            q_s[n * j:n * (j + 1), :] = q[:, HEAD_DIM * hh:HEAD_DIM * (hh + 1)]

    ones = jnp.ones((HEAD_DIM, HEAD_DIM), _BF16)

    def fold(t):
        return t, pltpu.roll(t, N_HEADS, axis=0)

    for s in range(blk):
        seq = i * blk + s
        row = pl.ds(seq, 1)
        q8 = q_s[pl.ds(seq, SUBLANES, stride=n), :]
        kq = (k_ref[s].reshape(groups, SUBLANES, HEAD_DIM) * q8[None]).astype(_BF16)
        sc = _dot(kq.reshape(KV_ROWS, HEAD_DIM), ones).reshape(groups, SUBLANES, HEAD_DIM)
        a, b2 = fold(jnp.max(sc, axis=0))
        e = jnp.exp(sc - jnp.maximum(a, b2)[None])
        a, b2 = fold(jnp.sum(e, axis=0))
        den = a + b2
        a, b2 = fold(jnp.sum(e * v_ref[s].reshape(groups, SUBLANES, HEAD_DIM), axis=0))
        o8 = (a + b2) / den
        for hh in range(N_HEADS):
            o_s[hh, row, :] = o8[hh:hh + 1, :]

    rows = pl.ds(pl.multiple_of(i * blk, blk), blk)
    acc = cst_ref[0] * cw_ref[0:1, :]
    for k in range(1, CONV_K - 1):
        acc = acc + cst_ref[k] * cw_ref[k:k + 1, :]
    acc_s[rows, :] = acc
    cnew_ref[0:CONV_K - 2] = cst_ref[1:CONV_K - 1]
    cnew_ref[CONV_K - 2] = vb_s[rows, :]

    @pl.when(i == pl.num_programs(1) - 1)
    def _post():
        h = h_s[...]
        acc = acc_s[...] + cb_ref[...] + cw_ref[CONV_K - 1:CONV_K, :] * vb_s[...]
        cb = _silu(_layernorm(acc, lng_ref[...], lnb_ref[...]))
        cb = cb * _silu(_dot(h, win_ref[:, C_ZB:C_Q]))
        m = m_s[...] + _sigmoid(_dot(h, win_ref[:, C_GB:C_GX])) * _dot(cb.astype(_BF16), wbrb_ref[...])
        ox = jnp.concatenate([o_s[hh] for hh in range(N_HEADS)], axis=-1)
        ox = ox * _silu(_dot(h, win_ref[:, C_ZX:C_GA]))
        m = m + _sigmoid(_dot(h, win_ref[:, C_GX:C_END])) * _dot(ox.astype(_BF16), wbrx_ref[...])
        out = x_s[...] + _dot(m.astype(_BF16), wout_ref[...])
        x_s[...] = out

        @pl.when(layer == pl.num_programs(0) - 1)
        def _final():
            y_ref[...] = _rmsnorm(out, fg_ref[...])


def _sample_trunk(x2d, p, s0re, s0im, cst, kc, vc, final_g, *, blk):
    n = x2d.shape[0]
    depth = cst.shape[0]
    kernel = functools.partial(_sample_kernel, n=n, blk=blk)

    def per_layer(arr):
        nd = arr.ndim - 1
        return pl.BlockSpec((None,) + arr.shape[1:], lambda l, i, _nd=nd: (l,) + (0,) * _nd,
                            pipeline_mode=pl.Buffered(1))

    kv_spec = pl.BlockSpec((None, blk, KV_ROWS, HEAD_DIM), lambda l, i: (l, i, 0, 0))
    win_spec = pl.BlockSpec((None, CONV_K - 1, blk, E_BR), lambda l, i: (l, 0, i, 0))
    mid = [p[nm] for nm in ('lam_re1', 'lam_im1', 'bmat_bf', 'cmat_bf', 'd', 'w_glu_bf', 'b_glu', 'w_br_ssm_bf',
                            'conv_w', 'conv_b', 'ln_g', 'ln_b', 'w_br_conv_bf')]
    ins = [x2d, p['norm_g'], p['w_in_bf']] + mid + [s0re, s0im, cst, kc, vc, p['w_br_xatt_bf'], p['w_out_bf'],
                                                    final_g]
    in_specs = ([_const_spec(x2d), per_layer(p['norm_g']), per_layer(p['w_in_bf'])] + [per_layer(a) for a in mid]
                + [per_layer(s0re), per_layer(s0im), win_spec, kv_spec, kv_spec, per_layer(p['w_br_xatt_bf']),
                   per_layer(p['w_out_bf']), _const_spec(final_g)])
    state_spec = pl.BlockSpec((None, n, N_SSM), lambda l, i: (l, 0, 0))
    return pl.pallas_call(
        kernel,
        grid=(depth, n // blk),
        in_specs=in_specs,
        out_specs=[pl.BlockSpec((n, D_MODEL), lambda l, i: (0, 0)), state_spec, state_spec, win_spec],
        out_shape=[jax.ShapeDtypeStruct((n, D_MODEL), _F32),
                   jax.ShapeDtypeStruct((depth, n, N_SSM), _F32),
                   jax.ShapeDtypeStruct((depth, n, N_SSM), _F32),
                   jax.ShapeDtypeStruct(cst.shape, _F32)],
        scratch_shapes=[pltpu.VMEM((n, D_MODEL), _F32),
                        pltpu.VMEM((n, D_MODEL), _BF16),
                        pltpu.VMEM((n, D_MODEL), _F32),
                        pltpu.VMEM((SUBLANES * n, HEAD_DIM), _F32),
                        pltpu.VMEM((N_HEADS, n, HEAD_DIM), _F32),
                        pltpu.VMEM((n, E_BR), _F32),
                        pltpu.VMEM((n, E_BR), _F32)],
        compiler_params=pltpu.CompilerParams(dimension_semantics=("arbitrary", "arbitrary"),
                                             vmem_limit_bytes=VMEM_LIMIT),
        name="sample_trunk",
    )(*ins)


def _prep(norm_g, w_in, a_re, a_im, log_dt, b_re, b_im, c_re, c_im, d, w_glu, b_glu, w_br_ssm,
          conv_w, conv_b, ln_g, ln_b, w_br_conv, w_br_xatt, w_out):
    depth = norm_g.shape[0]
    dt = jnp.exp(log_dt)[..., None]
    mag = jnp.exp(a_re * dt)
    lr = mag * jnp.cos(a_im * dt)
    li = mag * jnp.sin(a_im * dt)
    den = a_re * a_re + a_im * a_im
    cr = ((lr - 1.0) * a_re + li * a_im) / den
    ci = (li * a_re - (lr - 1.0) * a_im) / den
    bb_re = cr[..., None] * b_re - ci[..., None] * b_im
    bb_im = cr[..., None] * b_im + ci[..., None] * b_re
    gpb = N_GROUPS // N_BLK
    eye = jnp.eye(gpb, dtype=_F32)

    def b_blocks(bb):
        bb = bb.reshape(depth, N_BLK, gpb, N_STATE, GROUP)
        return jnp.einsum('ljgph,gk->ljghkp', bb, eye).reshape(depth, N_BLK, LANES, BLK_STATE)

    def c_blocks(cc):
        cc = cc.reshape(depth, N_BLK, gpb, GROUP, N_STATE)
        return jnp.einsum('ljghp,gk->ljkpgh', cc, eye).reshape(depth, N_BLK, BLK_STATE, LANES)

    bmat = jnp.concatenate([b_blocks(bb_re), b_blocks(bb_im)], axis=-1)
    cmat = jnp.concatenate([c_blocks(c_re), -c_blocks(c_im)], axis=2)
    lr1 = lr.reshape(depth, 1, N_SSM)
    li1 = li.reshape(depth, 1, N_SSM)
    row = lambda v: v.reshape(depth, 1, -1)
    return {
        'norm_g': row(norm_g), 'w_in_bf': w_in.astype(_BF16),
        'lam_re1': lr1, 'lam_im1': li1,
        'lam_re8': jnp.broadcast_to(lr1, (depth, SUBLANES, N_SSM)),
        'lam_im8': jnp.broadcast_to(li1, (depth, SUBLANES, N_SSM)),
        'bmat_bf': bmat.astype(_BF16), 'cmat_bf': cmat.astype(_BF16),
        'd': row(d), 'w_glu_bf': w_glu.astype(_BF16), 'b_glu': row(b_glu),
        'w_br_ssm_bf': w_br_ssm.astype(_BF16), 'conv_w': conv_w, 'conv_w8': jnp.repeat(conv_w, SUBLANES, axis=1),
        'conv_b': row(conv_b),
        'ln_g': row(ln_g), 'ln_b': row(ln_b), 'w_br_conv_bf': w_br_conv.astype(_BF16),
        'w_br_xatt_bf': w_br_xatt.astype(_BF16), 'w_out_bf': w_out.astype(_BF16),
    }


def kernel(x_prompt, x_sample, mem_prompt, state_ssm_re, state_ssm_im, state_conv, cache_mem_k, cache_mem_v,
           norm_g, w_in, ssm_a_re, ssm_a_im, ssm_log_dt, ssm_b_re, ssm_b_im, ssm_c_re, ssm_c_im, ssm_d,
           w_glu, b_glu, w_br_ssm, conv_w, conv_b, conv_ln_g, conv_ln_b, w_br_conv,
           mem_norm_g, w_k, w_v, w_br_xatt, w_out, final_norm_g):
    bp, seq, _ = x_prompt.shape
    bs = x_sample.shape[0]
    depth = norm_g.shape[0]
    assert bp == SUBLANES and x_sample.shape[1] == 1 and depth == DEPTH
    rows, chunk, sample_blk = PROMPT_TILE_ROWS, PROMPT_CHUNK_ROWS, SAMPLE_BLOCK

    p = _prep(norm_g, w_in, ssm_a_re, ssm_a_im, ssm_log_dt, ssm_b_re, ssm_b_im, ssm_c_re, ssm_c_im, ssm_d,
              w_glu, b_glu, w_br_ssm, conv_w, conv_b, conv_ln_g, conv_ln_b, w_br_conv, w_br_xatt, w_out)
    final_g = final_norm_g.reshape(1, D_MODEL)
    wkv = jnp.concatenate([w_k, w_v], axis=-1).astype(_BF16)
    mem_k, mem_v, kb, va = _memkv(mem_prompt, mem_norm_g.reshape(depth, 1, D_MODEL), wkv)

    xp = x_prompt
    xs = x_sample.reshape(bs, D_MODEL)
    s0re = state_ssm_re.reshape(depth, bs, N_SSM)
    s0im = state_ssm_im.reshape(depth, bs, N_SSM)
    kc = cache_mem_k.reshape(depth, bs, KV_ROWS, HEAD_DIM)
    vc = cache_mem_v.reshape(depth, bs, KV_ROWS, HEAD_DIM)

    p_re, p_im, p_conv = [], [], []
    for l in range(depth):
        final = l == depth - 1
        xp, re, im, tail = _prompt_layer(xp, p, l, kb, va, final_g, rows=rows, chunk=chunk, final=final)
        p_re.append(re.reshape(bp, N_GROUPS, N_STATE))
        p_im.append(im.reshape(bp, N_GROUPS, N_STATE))
        p_conv.append(tail.reshape(CONV_K - 1, bp, E_BR).transpose(1, 0, 2))

    ys, s_re, s_im, s_conv = _sample_trunk(xs, p, s0re, s0im, state_conv.transpose(0, 2, 1, 3), kc, vc, final_g,
                                           blk=sample_blk)
    s_conv = s_conv.transpose(0, 2, 1, 3)

    y_prompt = xp
    y_sample = ys.reshape(bs, 1, D_MODEL)
    ssm_shape = (depth, bs, N_GROUPS, N_STATE)
    kv_shape = (depth, bp, N_MEM, N_HEADS, HEAD_DIM)
    return (y_prompt, y_sample, jnp.stack(p_re), jnp.stack(p_im), jnp.stack(p_conv), mem_k.reshape(kv_shape),
            mem_v.reshape(kv_shape), s_re.reshape(ssm_shape), s_im.reshape(ssm_shape), s_conv)
```
